```python
import math
import jax, jax.numpy as jnp
from jax import lax
import numpy as np

D_MODEL = 1024
BATCH = 8
SEQ = 4096
DEPTH = 2

HEAD_DIM = 64
A_Q_HEADS = 8
A_KV_HEADS = 2
WINDOW = 128
ROPE_DIM = HEAD_DIM // 4
ROPE_THETA = 500000.0
B_HEADS = 4
B_DK = 64
B_DV = 64
C_HEADS = 4
C_DK = 64
C_DV = 64
CONV_WIDTH = 4
CHUNK = 64
D_FF = -(-8 * D_MODEL // (3 * 256)) * 256
D_MIX = A_Q_HEADS * HEAD_DIM + B_HEADS * B_DV + C_HEADS * C_DV
C_CONV_CH = C_HEADS * (2 * C_DK + C_DV)
IN_SIZES = (A_Q_HEADS * HEAD_DIM, A_KV_HEADS * HEAD_DIM, A_KV_HEADS * HEAD_DIM,
            B_HEADS * B_DK, B_HEADS * B_DK, B_HEADS * B_DV, B_HEADS * B_DV,
            C_CONV_CH, C_HEADS * C_DV, C_HEADS, C_HEADS)
D_IN = sum(IN_SIZES)
NORM_EPS = 1e-6

kernel_name = 'hybrid_swa_hgrn2_gdn_block'


def rmsnorm(x, gain):
    xf = x.astype(jnp.float32)
    y = xf * lax.rsqrt(jnp.mean(xf * xf, axis=-1, keepdims=True) + NORM_EPS)
    return (y * gain.astype(jnp.float32)).astype(x.dtype)


def l2norm(x):
    return x * lax.rsqrt(jnp.sum(x * x, axis=-1, keepdims=True) + NORM_EPS)


def partial_rope(x, positions):
    half = ROPE_DIM // 2
    inv_freq = ROPE_THETA ** (-jnp.arange(half, dtype=jnp.float32) * 2.0 / ROPE_DIM)
    ang = positions.astype(jnp.float32)[..., None] * inv_freq
    cos = jnp.cos(ang)[:, :, None, :]
    sin = jnp.sin(ang)[:, :, None, :]
    xf = x.astype(jnp.float32)
    x1, x2, xp = xf[..., :half], xf[..., half:ROPE_DIM], xf[..., ROPE_DIM:]
    out = jnp.concatenate([x1 * cos - x2 * sin, x2 * cos + x1 * sin, xp], axis=-1)
    return out.astype(x.dtype)


def sliding_window_attention(q, k, v, sinks):
    b, s, hq, d = q.shape
    hkv = k.shape[2]
    grp = hq // hkv
    nb = s // WINDOW
    qb = q.reshape(b, nb, WINDOW, hkv, grp, d)

    def band(t):
        tb = t.reshape(b, nb, WINDOW, hkv, d)
        prev = jnp.concatenate([jnp.zeros_like(tb[:, :1]), tb[:, :-1]], axis=1)
        return jnp.concatenate([prev, tb], axis=2)

    kb, vb = band(k), band(v)
    scores = jnp.einsum('bnqhgd,bnkhd->bnhgqk', qb, kb).astype(jnp.float32) * (d ** -0.5)
    qi = jnp.arange(WINDOW)[:, None]
    kj = jnp.arange(2 * WINDOW)[None, :]
    delta = qi + WINDOW - kj
    blk = jnp.arange(nb)[:, None, None]
    valid = (delta >= 0) & (delta < WINDOW) & (blk * WINDOW + kj - WINDOW >= 0)
    scores = jnp.where(valid[None, :, None, None], scores, -jnp.inf)
    sink = sinks.astype(jnp.float32).reshape(hkv, grp)[None, None, :, :, None, None]
    m = jnp.maximum(scores.max(axis=-1, keepdims=True), sink)
    p = jnp.exp(scores - m)
    denom = p.sum(axis=-1, keepdims=True) + jnp.exp(sink - m)
    out = jnp.einsum('bnhgqk,bnkhd->bnqhgd', (p / denom).astype(v.dtype), vb)
    return out.reshape(b, s, hq * d)


def hgrn2_chunked(q, log_f, v):
    b, s, h, dk = q.shape
    dv = v.shape[-1]
    nc = s // CHUNK

    def to_chunks(t):
        return t.reshape(b, nc, CHUNK, h, t.shape[-1]).transpose(1, 0, 3, 2, 4)

    causal = jnp.tril(jnp.ones((CHUNK, CHUNK), dtype=bool))

    def step(state, inp):
        qt, lf, vt = inp
        kt = -jnp.expm1(lf)
        cum = jnp.cumsum(lf, axis=2)
        diff = cum[:, :, :, None, :] - cum[:, :, None, :, :]
        decay = jnp.exp(jnp.where(causal[:, :, None], diff, -jnp.inf))
        attn = jnp.einsum('bhtd,bhsd,bhtsd->bhts', qt, kt, decay)
        o = attn @ vt + jnp.einsum('bhtd,bhde->bhte', qt * jnp.exp(cum), state)
        last = cum[:, :, -1:, :]
        state = jnp.exp(last[:, :, 0, :, None]) * state + jnp.einsum('bhsd,bhse->bhde', kt * jnp.exp(last - cum), vt)
        return state, o

    state0 = jnp.zeros((b, h, dk, dv), jnp.float32)
    _, o = lax.scan(step, state0, (to_chunks(q), to_chunks(log_f), to_chunks(v)))
    return o.transpose(1, 0, 3, 2, 4).reshape(b, s, h, dv)


def causal_depthwise_conv(x, w):
    ch = x.shape[-1]
    return lax.conv_general_dilated(x, w[:, None, :].astype(x.dtype), window_strides=(1,),
                                    padding=[(CONV_WIDTH - 1, 0)],
                                    dimension_numbers=('NWC', 'WIO', 'NWC'),
                                    feature_group_count=ch)


def gated_delta_chunked(q, k, v, beta, g):
    b, s, h, dk = q.shape
    dv = v.shape[-1]
    nc = s // CHUNK

    def to_chunks(t):
        return jnp.swapaxes(t.reshape((b, nc, CHUNK) + t.shape[2:]), 2, 3)

    qc, kc, vc, bc, gc = (to_chunks(t) for t in (q, k, v, beta, g))
    G = jnp.cumsum(gc, axis=-1)
    incl = jnp.tril(jnp.ones((CHUNK, CHUNK), dtype=bool))
    strict = jnp.tril(jnp.ones((CHUNK, CHUNK), dtype=bool), k=-1)
    L = jnp.exp(jnp.where(incl, G[..., :, None] - G[..., None, :], -jnp.inf))
    kb = kc * bc[..., None]
    A = jnp.where(strict, jnp.einsum('bnhtd,bnhsd->bnhts', kb, kc) * L, 0.0)
    eye = jnp.eye(CHUNK, dtype=jnp.float32)
    T = lax.linalg.triangular_solve(eye + A, jnp.broadcast_to(eye, A.shape), left_side=True,
                                    lower=True, unit_diagonal=True)
    U = T @ (vc * bc[..., None])
    W = T @ (kb * jnp.exp(G)[..., None])
    qk = jnp.where(incl, jnp.einsum('bnhtd,bnhsd->bnhts', qc, kc) * L, 0.0)
    q_dec = qc * jnp.exp(G)[..., None]
    k_dec = kc * jnp.exp(G[..., -1:] - G)[..., None]
    g_last = jnp.exp(G[..., -1])

    def step(state, inp):
        u, w, qd, kd, qkc, gl = inp
        v_new = u - w @ state
        o = qd @ state + qkc @ v_new
        state = gl[..., None, None] * state + jnp.swapaxes(kd, -1, -2) @ v_new
        return state, o

    xs = tuple(jnp.moveaxis(t, 1, 0) for t in (U, W, q_dec, k_dec, qk, g_last))
    state0 = jnp.zeros((b, h, dk, dv), jnp.float32)
    _, o = lax.scan(step, state0, xs)
    return jnp.swapaxes(jnp.moveaxis(o, 0, 1), 2, 3).reshape(b, s, h, dv)


def hybrid_mixer(h, positions, w_in, q_norm, k_norm, sinks, lb, hgrn_norm, conv_w, a_log, dt_bias,
                 gdn_norm, w_out):
    b, s, _ = h.shape
    f32 = jnp.float32
    proj = h @ w_in
    split_at = [int(i) for i in np.cumsum(IN_SIZES)[:-1]]
    aq, ak, av, bq, bf, bv, bg, cqkv, cg, cb, ca = jnp.split(proj, split_at, axis=-1)

    aq = partial_rope(rmsnorm(aq.reshape(b, s, A_Q_HEADS, HEAD_DIM), q_norm), positions)
    ak = partial_rope(rmsnorm(ak.reshape(b, s, A_KV_HEADS, HEAD_DIM), k_norm), positions)
    av = av.reshape(b, s, A_KV_HEADS, HEAD_DIM)
    out_a = sliding_window_attention(aq, ak, av, sinks)

    lb = lb.astype(f32).reshape(B_HEADS, B_DK)
    z = bf.reshape(b, s, B_HEADS, B_DK).astype(f32)
    log_f = jnp.logaddexp(jnp.log(lb), jnp.log1p(-lb) + jax.nn.log_sigmoid(z))
    o_b = hgrn2_chunked(bq.reshape(b, s, B_HEADS, B_DK).astype(f32), log_f,
                        bv.reshape(b, s, B_HEADS, B_DV).astype(f32))
    out_b = rmsnorm(o_b, hgrn_norm) * jax.nn.silu(bg.reshape(b, s, B_HEADS, B_DV).astype(f32))
    out_b = out_b.reshape(b, s, B_HEADS * B_DV).astype(h.dtype)

    cqkv = jax.nn.silu(causal_depthwise_conv(cqkv, conv_w))
    cq, ck, cv = jnp.split(cqkv, [C_HEADS * C_DK, 2 * C_HEADS * C_DK], axis=-1)
    cq = l2norm(cq.reshape(b, s, C_HEADS, C_DK).astype(f32)) * (C_DK ** -0.5)
    ck = l2norm(ck.reshape(b, s, C_HEADS, C_DK).astype(f32))
    cv = cv.reshape(b, s, C_HEADS, C_DV).astype(f32)
    beta = jax.nn.sigmoid(cb.astype(f32))
    g = -jnp.exp(a_log.astype(f32)) * jax.nn.softplus(ca.astype(f32) + dt_bias.astype(f32))
    o_c = gated_delta_chunked(cq, ck, cv, beta, g)
    out_c = rmsnorm(o_c, gdn_norm) * jax.nn.silu(cg.reshape(b, s, C_HEADS, C_DV).astype(f32))
    out_c = out_c.reshape(b, s, C_HEADS * C_DV).astype(h.dtype)

    mixed = jnp.concatenate([out_a, out_b, out_c], axis=-1)
    return mixed @ w_out


def swiglu(h, w_gate, w_up, w_down):
    return (jax.nn.silu(h @ w_gate) * (h @ w_up)) @ w_down


def setup_inputs(seed: int = 0) -> dict:
    key = jax.random.key(seed)
    ks = jax.random.split(key, 24)
    f32 = jnp.float32

    def nrm(k, shape, scale):
        return jax.random.normal(k, shape, f32) * scale

    x = nrm(ks[0], (BATCH, SEQ, D_MODEL), 1.0)
    c = nrm(ks[1], (BATCH, D_MODEL), 1.0)
    positions = jnp.broadcast_to(jnp.arange(SEQ, dtype=jnp.int32)[None, :], (BATCH, SEQ))
    ada_w = nrm(ks[2], (DEPTH, D_MODEL, 6 * D_MODEL), 0.5 * D_MODEL ** -0.5)
    ada_b = nrm(ks[3], (DEPTH, 6 * D_MODEL), 0.02)
    norm_mix = 1.0 + nrm(ks[4], (DEPTH, D_MODEL), 0.05)
    w_in = nrm(ks[5], (DEPTH, D_MODEL, D_IN), D_MODEL ** -0.5)
    attn_q_norm = 1.0 + nrm(ks[6], (DEPTH, HEAD_DIM), 0.05)
    attn_k_norm = 1.0 + nrm(ks[7], (DEPTH, HEAD_DIM), 0.05)
    attn_sinks = nrm(ks[8], (DEPTH, A_Q_HEADS), 1.0)
    hgrn_lb_logits = nrm(ks[9], (DEPTH, B_HEADS * B_DK), 1.0)
    hgrn_out_norm = 1.0 + nrm(ks[10], (DEPTH, B_DV), 0.05)
    gdn_conv_w = nrm(ks[11], (DEPTH, CONV_WIDTH, C_CONV_CH), CONV_WIDTH ** -0.5)
    gdn_a_log = jnp.log(jax.random.uniform(ks[12], (DEPTH, C_HEADS), f32, 1.0, 16.0))
    dt = jnp.exp(jax.random.uniform(ks[13], (DEPTH, C_HEADS), f32, math.log(1e-3), math.log(1e-1)))
    gdn_dt_bias = dt + jnp.log(-jnp.expm1(-dt))
    gdn_out_norm = 1.0 + nrm(ks[14], (DEPTH, C_DV), 0.05)
    w_out = nrm(ks[15], (DEPTH, D_MIX, D_MODEL), D_MIX ** -0.5)
    norm_ffn = 1.0 + nrm(ks[16], (DEPTH, D_MODEL), 0.05)
    w_gate = nrm(ks[17], (DEPTH, D_MODEL, D_FF), D_MODEL ** -0.5)
    w_up = nrm(ks[18], (DEPTH, D_MODEL, D_FF), D_MODEL ** -0.5)
    w_down = nrm(ks[19], (DEPTH, D_FF, D_MODEL), D_FF ** -0.5)
    return {'x': x, 'c': c, 'positions': positions, 'ada_w': ada_w, 'ada_b': ada_b,
            'norm_mix': norm_mix, 'w_in': w_in, 'attn_q_norm': attn_q_norm,
            'attn_k_norm': attn_k_norm, 'attn_sinks': attn_sinks, 'hgrn_lb_logits': hgrn_lb_logits,
            'hgrn_out_norm': hgrn_out_norm, 'gdn_conv_w': gdn_conv_w, 'gdn_a_log': gdn_a_log,
            'gdn_dt_bias': gdn_dt_bias, 'gdn_out_norm': gdn_out_norm, 'w_out': w_out,
            'norm_ffn': norm_ffn, 'w_gate': w_gate, 'w_up': w_up, 'w_down': w_down}


def reference(x, c, positions, ada_w, ada_b, norm_mix, w_in, attn_q_norm, attn_k_norm, attn_sinks,
              hgrn_lb_logits, hgrn_out_norm, gdn_conv_w, gdn_a_log, gdn_dt_bias, gdn_out_norm, w_out,
              norm_ffn, w_gate, w_up, w_down):
    lb_cum = jnp.cumsum(jax.nn.softmax(hgrn_lb_logits.astype(jnp.float32), axis=0), axis=0)
    lower_bounds = lb_cum - lb_cum[:1]
    cond = jax.nn.silu(c)
    for l in range(DEPTH):
        mod = cond @ ada_w[l] + ada_b[l]
        sh_m, sc_m, gt_m, sh_f, sc_f, gt_f = [m[:, None, :] for m in jnp.split(mod, 6, axis=-1)]
        h = rmsnorm(x, norm_mix[l]) * (1 + sc_m) + sh_m
        y = hybrid_mixer(h, positions, w_in[l], attn_q_norm[l], attn_k_norm[l], attn_sinks[l],
                         lower_bounds[l], hgrn_out_norm[l], gdn_conv_w[l], gdn_a_log[l],
                         gdn_dt_bias[l], gdn_out_norm[l], w_out[l])
        x = x + gt_m * y
        h = rmsnorm(x, norm_ffn[l]) * (1 + sc_f) + sh_f
        x = x + gt_f * swiglu(h, w_gate[l], w_up[l], w_down[l])
    return x
```

```python
import functools
import math

import numpy as np
import jax
import jax.numpy as jnp
from jax import lax
from jax.experimental import pallas as pl
from jax.experimental.pallas import tpu as pltpu

F32 = jnp.float32
BF16 = jnp.bfloat16

D_MODEL = 1024
HEAD_DIM = 64
A_Q_HEADS = 8
A_KV_HEADS = 2
WINDOW = 128
ROPE_DIM = HEAD_DIM // 4
ROPE_THETA = 500000.0
B_HEADS = 4
C_HEADS = 4
CONV_WIDTH = 4
CHUNK = 64
D_FF = 2816
D_MIX = 1024
D_IN = 2824
NORM_EPS = 1e-6

LANES = 128
SUBLANES = 8
D_IN_PAD = 2944
GATE_COL = 2816
VMEM_LIMIT = 56 * 1024 * 1024

NN = (((1,), (0,)), ((), ()))
NT = (((1,), (1,)), ((), ()))
TN = (((0,), (0,)), ((), ()))


def _dot(a, b, dims=NN):
    return lax.dot_general(a, b, dims, preferred_element_type=F32)


def _split(x, n):
    parts = []
    r = x
    for i in range(n):
        p = r.astype(BF16)
        parts.append(p)
        if i + 1 < n:
            r = r - p.astype(F32)
    return parts


def _dot_exact_l(m, x, n=3):
    out = None
    for p in _split(x, n):
        t = _dot(m, p)
        out = t if out is None else out + t
    return out


def _dot_exact_r(x, m, n=3):
    out = None
    for p in _split(x, n):
        t = _dot(p, m)
        out = t if out is None else out + t
    return out


def _dot3(a, b, dims=NN):
    a1, a2 = _split(a, 2)
    b1, b2 = _split(b, 2)
    return _dot(a1, b1, dims) + _dot(a1, b2, dims) + _dot(a2, b1, dims)


def _silu(x):
    return x * jax.nn.sigmoid(x)


def _lane_lo(shape):
    return (lax.broadcasted_iota(jnp.int32, shape, len(shape) - 1) & (LANES - 1)) < HEAD_DIM


def _blockdiag2(x):
    lo = _lane_lo(x.shape)
    zero = jnp.zeros_like(x)
    return jnp.concatenate([jnp.where(lo, x, zero), jnp.where(lo, zero, x)], axis=0)


def _group_sumsq(x, bd):
    return _dot_exact_r(x * x, bd, n=2)


def _mod_kernel(c_ref, w_ref, b_ref, o_ref):
    c = c_ref[...]
    o_ref[...] = _dot3(_silu(c), w_ref[...]) + b_ref[...]


def _modulation(c, ada_w, ada_b):
    depth = ada_w.shape[0]
    b = c.shape[0]
    return pl.pallas_call(
        _mod_kernel,
        grid=(depth, 6),
        in_specs=[
            pl.BlockSpec((b, D_MODEL), lambda l, k: (0, 0)),
            pl.BlockSpec((None, D_MODEL, D_MODEL), lambda l, k: (l, 0, k)),
            pl.BlockSpec((None, None, 1, D_MODEL), lambda l, k: (l, k, 0, 0)),
        ],
        out_specs=pl.BlockSpec((None, None, b, D_MODEL), lambda l, k: (l, k, 0, 0)),
        out_shape=jax.ShapeDtypeStruct((depth, 6, b, D_MODEL), F32),
        compiler_params=pltpu.CompilerParams(
            dimension_semantics=("arbitrary", "arbitrary"), vmem_limit_bytes=VMEM_LIMIT),
        name="modulation",
    )(c, ada_w, ada_b.reshape(depth, 6, 1, D_MODEL))


def _rope_kernel(pos_ref, inv_ref, sgn_ref, cos_ref, sin_ref):
    ang = pos_ref[...].astype(F32) * inv_ref[...]
    cos_ref[...] = jnp.cos(ang)
    sin_ref[...] = jnp.sin(ang) * sgn_ref[...]


def _rope_tables(positions):
    n = positions.size
    tm = min(2048, n)
    half = ROPE_DIM // 2
    inv_freq = ROPE_THETA ** (-jnp.arange(half, dtype=F32) * 2.0 / ROPE_DIM)
    lane = np.arange(LANES) % HEAD_DIM
    inv_lane = jnp.where(lane < ROPE_DIM, inv_freq[lane % half], 0.0).reshape(1, LANES).astype(F32)
    sgn_lane = jnp.asarray(np.where(lane < half, -1.0, np.where(lane < ROPE_DIM, 1.0, 0.0)),
                           F32).reshape(1, LANES)
    return pl.pallas_call(
        _rope_kernel,
        grid=(n // tm,),
        in_specs=[
            pl.BlockSpec((tm, 1), lambda i: (i, 0)),
            pl.BlockSpec((1, LANES), lambda i: (0, 0)),
            pl.BlockSpec((1, LANES), lambda i: (0, 0)),
        ],
        out_specs=[pl.BlockSpec((tm, LANES), lambda i: (i, 0))] * 2,
        out_shape=[jax.ShapeDtypeStruct((n, LANES), F32)] * 2,
        compiler_params=pltpu.CompilerParams(dimension_semantics=("arbitrary",)),
        name="rope_tables",
    )(positions.reshape(n, 1), inv_lane, sgn_lane)


def _modulated_norm(x, gain, scale, shift):
    ms = jnp.mean(x * x, axis=-1, keepdims=True)
    return x * lax.rsqrt(ms + NORM_EPS) * gain * (1.0 + scale) + shift


def _inproj_kernel(x_ref, g_ref, sc_ref, sh_ref, w_ref, o_ref):
    h = _modulated_norm(x_ref[...], g_ref[...], sc_ref[...], sh_ref[...])
    o_ref[...] = _dot(h.astype(BF16), w_ref[...])


def _mod_spec(layer, which, tiles_per_batch):
    return pl.BlockSpec((None, None, None, 1, D_MODEL),
                        lambda i: (layer, which, i // tiles_per_batch, 0, 0))


def _inproj(x2, mod5, gain, w_pad, layer, seq):
    n = x2.shape[0]
    tm = min(512, seq)
    tpb = seq // tm
    return pl.pallas_call(
        _inproj_kernel,
        grid=(n // tm,),
        in_specs=[
            pl.BlockSpec((tm, D_MODEL), lambda i: (i, 0)),
            pl.BlockSpec((1, D_MODEL), lambda i: (0, 0)),
            _mod_spec(layer, 1, tpb),
            _mod_spec(layer, 0, tpb),
            pl.BlockSpec((D_MODEL, D_IN_PAD), lambda i: (0, 0)),
        ],
        out_specs=pl.BlockSpec((tm, D_IN_PAD), lambda i: (i, 0)),
        out_shape=jax.ShapeDtypeStruct((n, D_IN_PAD), F32),
        compiler_params=pltpu.CompilerParams(
            dimension_semantics=("arbitrary",), vmem_limit_bytes=VMEM_LIMIT),
        name="inproj",
    )(x2, gain, mod5, mod5, w_pad)


def _attn_kernel(sink_ref, q_ref, k_ref, v_ref, cos_ref, sin_ref, qg_ref, kg_ref, bd_ref,
                 o_ref, kvar_ref, vvar_ref):
    n = pl.program_id(1)
    w = WINDOW

    @pl.when(n == 0)
    def _():
        kvar_ref[:, 0:w, :] = jnp.zeros((4, w, LANES), BF16)
        vvar_ref[:, 0:w, :] = jnp.zeros((4, w, LANES), BF16)

    @pl.when(n > 0)
    def _():
        kvar_ref[:, 0:w, :] = kvar_ref[:, w:2 * w, :]
        vvar_ref[:, 0:w, :] = vvar_ref[:, w:2 * w, :]

    cos = cos_ref[...]
    sin = sin_ref[...]
    bd = bd_ref[...]
    lane = lax.broadcasted_iota(jnp.int32, (w, LANES), 1)
    lo = lane < HEAD_DIM
    take_next = (lane & (HEAD_DIM - 1)) < (ROPE_DIM // 2)

    def norm_rope(xp, gain):
        ss = _group_sumsq(xp, bd)
        y = xp * lax.rsqrt(ss * (1.0 / HEAD_DIM) + NORM_EPS) * gain
        other = jnp.where(take_next, pltpu.roll(y, LANES - ROPE_DIM // 2, 1),
                          pltpu.roll(y, ROPE_DIM // 2, 1))
        return y * cos + other * sin

    def variants(x):
        xs = pltpu.roll(x, HEAD_DIM, 1)
        zero = jnp.zeros_like(x)
        return [jnp.where(lo, x, zero), jnp.where(lo, zero, xs),
                jnp.where(lo, xs, zero), jnp.where(lo, zero, x)]

    kn = norm_rope(k_ref[...], kg_ref[...])
    for i, t in enumerate(variants(kn)):
        kvar_ref[i, w:2 * w, :] = t.astype(BF16)
    for i, t in enumerate(variants(v_ref[...])):
        vvar_ref[i, w:2 * w, :] = t.astype(BF16)

    qi = lax.broadcasted_iota(jnp.int32, (w, 2 * w), 0)
    kj = lax.broadcasted_iota(jnp.int32, (w, 2 * w), 1)
    delta = qi + w - kj
    valid = (delta >= 0) & (delta < w) & ((n * w + kj - w) >= 0)

    for p in range(A_Q_HEADS // 2):
        g = (2 * p) // (A_Q_HEADS // A_KV_HEADS)
        qn = norm_rope(q_ref[:, p * LANES:(p + 1) * LANES], qg_ref[...]) * (HEAD_DIM ** -0.5)
        qb = qn.astype(BF16)
        out = None
        for par in range(2):
            sink = sink_ref[2 * p + par]
            s = _dot(qb, kvar_ref[2 * g + par], NT)
            s = jnp.where(valid, s, -jnp.inf)
            m = jnp.maximum(jnp.max(s, axis=-1, keepdims=True), sink)
            e = jnp.exp(s - m)
            den = jnp.sum(e, axis=-1, keepdims=True) + jnp.exp(sink - m)
            o = _dot(e.astype(BF16), vvar_ref[2 * g + par]) / den
            out = o if out is None else out + o
        o_ref[:, p * LANES:(p + 1) * LANES] = out.astype(o_ref.dtype)


def _attention(proj, cos_t, sin_t, q_norm, k_norm, sinks, bd128, batch, seq):
    n = proj.shape[0]
    nb = seq // WINDOW
    qw = A_Q_HEADS * HEAD_DIM
    row = lambda b, i: b * nb + i
    qg = jnp.tile(q_norm.astype(F32), 2).reshape(1, LANES)
    kg = jnp.tile(k_norm.astype(F32), 2).reshape(1, LANES)
    return pl.pallas_call(
        _attn_kernel,
        grid=(batch, nb),
        in_specs=[
            pl.BlockSpec(memory_space=pltpu.SMEM),
            pl.BlockSpec((WINDOW, qw), lambda b, i: (row(b, i), 0)),
            pl.BlockSpec((WINDOW, LANES), lambda b, i: (row(b, i), qw // LANES)),
            pl.BlockSpec((WINDOW, LANES), lambda b, i: (row(b, i), qw // LANES + 1)),
            pl.BlockSpec((WINDOW, LANES), lambda b, i: (row(b, i), 0)),
            pl.BlockSpec((WINDOW, LANES), lambda b, i: (row(b, i), 0)),
            pl.BlockSpec((1, LANES), lambda b, i: (0, 0)),
            pl.BlockSpec((1, LANES), lambda b, i: (0, 0)),
            pl.BlockSpec((LANES, LANES), lambda b, i: (0, 0)),
        ],
        out_specs=pl.BlockSpec((WINDOW, qw), lambda b, i: (row(b, i), 0)),
        out_shape=jax.ShapeDtypeStruct((n, qw), BF16),
        scratch_shapes=[pltpu.VMEM((4, 2 * WINDOW, LANES), BF16),
                        pltpu.VMEM((4, 2 * WINDOW, LANES), BF16)],
        compiler_params=pltpu.CompilerParams(dimension_semantics=("arbitrary", "arbitrary")),
        name="attention",
    )(sinks.astype(F32), proj, proj, proj, cos_t, sin_t, qg, kg, bd128)


def _hgrn_consts():
    c = CHUNK
    t = np.arange(c)[:, None]
    r = np.arange(c)[None, :]
    mats = [(r <= t).astype(np.float32), (r > t).astype(np.float32)]
    for l in range(6):
        ref = ((t >> (l + 1)) << (l + 1)) + (1 << l) - 1
        mats.append((r <= t).astype(np.float32) - (r <= ref).astype(np.float32))
    mall = np.concatenate(mats, axis=0)
    s = np.arange(LANES)[None, :] % c
    x = t ^ s
    lvl = np.where(t > s, np.floor(np.log2(np.maximum(x, 1))).astype(np.int32),
                   np.where(t == s, -1, -2)).astype(np.int32)
    return jnp.asarray(mall, BF16), jnp.asarray(lvl, jnp.int32)


def _hgrn_kernel(q_ref, z_ref, v_ref, gt_ref, lbl_ref, gain_ref, mall_ref, lvl_ref, bd_ref,
                 o_ref, st_ref, *, layer):
    c = CHUNK
    nh = B_HEADS * HEAD_DIM

    @pl.when(pl.program_id(1) == 0)
    def _():
        st_ref[...] = jnp.zeros_like(st_ref)

    q = q_ref[...]
    z = z_ref[...]
    v = v_ref[...]

    lg = lbl_ref[...]
    e = jnp.exp(lg - jnp.max(lg, axis=0, keepdims=True))
    sm = e / jnp.sum(e, axis=0, keepdims=True)
    lb = jnp.zeros((1, nh), F32)
    for j in range(1, layer + 1):
        lb = lb + sm[j:j + 1, :]

    a = jnp.log(lb)
    b = jnp.log1p(-lb) + jnp.minimum(z, 0.0) - jnp.log1p(jnp.exp(-jnp.abs(z)))
    lf = jnp.maximum(a, b) + jnp.log1p(jnp.exp(-jnp.abs(a - b)))
    kk = (1.0 - lb) * jax.nn.sigmoid(-z)

    dall = _dot_exact_l(mall_ref[...], lf)
    cum = dall[0:c]
    rest = dall[c:2 * c]

    lvl = lvl_ref[...]
    trow = lax.broadcasted_iota(jnp.int32, (c, LANES), 0)
    outs = []
    for p in range(B_HEADS // 2):
        sl = slice(p * LANES, (p + 1) * LANES)
        qp, kp, vp = q[:, sl], kk[:, sl], v[:, sl]
        zero = jnp.zeros((c, LANES), F32)
        pm = jnp.where(lvl == -1, _dot(qp.astype(BF16), _blockdiag2(kp.astype(BF16)), NT), zero)
        for l in range(6):
            dec = jnp.exp(-jnp.abs(dall[(2 + l) * c:(3 + l) * c, sl]))
            is_q = ((trow >> l) & 1) == 1
            xl = (jnp.where(is_q, qp, kp) * dec).astype(BF16)
            pm = pm + jnp.where(lvl == l, _dot(xl, _blockdiag2(xl), NT), zero)
        outs.append(_dot(pm.astype(BF16), _blockdiag2(vp.astype(BF16))))
    o_intra = jnp.concatenate(outs, axis=1)

    st = st_ref[...]
    q_dec = (q * jnp.exp(cum)).astype(BF16)
    k_dec = (kk * jnp.exp(rest)).astype(BF16)
    o = o_intra + _dot(q_dec, st.astype(BF16), NT)
    upd = _dot(v.astype(BF16), k_dec, TN)
    ri = lax.broadcasted_iota(jnp.int32, (nh, nh), 0) // HEAD_DIM
    ci = lax.broadcasted_iota(jnp.int32, (nh, nh), 1) // HEAD_DIM
    st_ref[...] = st * jnp.exp(cum[c - 1:c, :]) + jnp.where(ri == ci, upd, jnp.zeros_like(upd))

    ss = _group_sumsq(o, bd_ref[...])
    y = o * lax.rsqrt(ss * (1.0 / HEAD_DIM) + NORM_EPS) * gain_ref[...]
    o_ref[...] = (y * _silu(gt_ref[...])).astype(o_ref.dtype)


def _hgrn(proj, lb_logits, gain, bd256, layer, batch, seq):
    n = proj.shape[0]
    nc = seq // CHUNK
    nh = B_HEADS * HEAD_DIM
    depth = lb_logits.shape[0]
    mall, lvl = _hgrn_consts()
    col0 = (A_Q_HEADS + 2 * A_KV_HEADS) * HEAD_DIM // nh
    row = lambda b, i: b * nc + i
    const = lambda shape: pl.BlockSpec(shape, lambda b, i: (0, 0))
    return pl.pallas_call(
        functools.partial(_hgrn_kernel, layer=layer),
        grid=(batch, nc),
        in_specs=[pl.BlockSpec((CHUNK, nh), lambda b, i, k=k: (row(b, i), col0 + k)) for k in range(4)]
        + [const((depth, nh)), const((1, nh)), const(mall.shape), const(lvl.shape), const((nh, nh))],
        out_specs=pl.BlockSpec((CHUNK, nh), lambda b, i: (row(b, i), 0)),
        out_shape=jax.ShapeDtypeStruct((n, nh), BF16),
        scratch_shapes=[pltpu.VMEM((nh, nh), F32)],
        compiler_params=pltpu.CompilerParams(dimension_semantics=("arbitrary", "arbitrary")),
        name="hgrn2",
    )(proj, proj, proj, proj, lb_logits.astype(F32), jnp.tile(gain.astype(F32), B_HEADS).reshape(1, nh),
      mall, lvl, bd256)


def _gdn_consts():
    c = CHUNK
    t = np.arange(c)[:, None]
    r = np.arange(c)[None, :]
    tri2 = np.concatenate([(r <= t), (r > t)], axis=0).astype(np.float32)
    nh = C_HEADS * HEAD_DIM
    ex = np.zeros((LANES, 2 * nh), np.float32)
    for h in range(C_HEADS):
        ex[h, h * HEAD_DIM:(h + 1) * HEAD_DIM] = 1.0
        ex[C_HEADS + h, nh + h * HEAD_DIM:nh + (h + 1) * HEAD_DIM] = 1.0
    return jnp.asarray(tri2, BF16), jnp.asarray(ex, BF16)


def _gdn_kernel(cq_ref, ck_ref, cv_ref, cg_ref, gate_ref, cw_ref, alog_ref, dt_ref, gain_ref,
                tri_ref, ex_ref, bd_ref, o_ref, xbuf_ref, s_ref):
    c = CHUNK
    nh = C_HEADS * HEAD_DIM
    tail = SUBLANES

    @pl.when(pl.program_id(1) == 0)
    def _():
        s_ref[...] = jnp.zeros_like(s_ref)
        xbuf_ref[0:tail, :] = jnp.zeros((tail, 3 * nh), F32)

    xbuf_ref[tail:tail + c, 0:nh] = cq_ref[...]
    xbuf_ref[tail:tail + c, nh:2 * nh] = ck_ref[...]
    xbuf_ref[tail:tail + c, 2 * nh:3 * nh] = cv_ref[...]
    w = cw_ref[...]
    y = None
    for j in range(CONV_WIDTH):
        off = tail - (CONV_WIDTH - 1) + j
        t = xbuf_ref[off:off + c, :] * w[j:j + 1, :]
        y = t if y is None else y + t
    xbuf_ref[0:tail, :] = xbuf_ref[c:c + tail, :]
    y = _silu(y)
    q, k, v = y[:, 0:nh], y[:, nh:2 * nh], y[:, 2 * nh:3 * nh]

    bd = bd_ref[...]
    q = q * lax.rsqrt(_group_sumsq(q, bd) + NORM_EPS) * (HEAD_DIM ** -0.5)
    k = k * lax.rsqrt(_group_sumsq(k, bd) + NORM_EPS)

    gates = _dot_exact_r(gate_ref[...], ex_ref[...])
    beta = jax.nn.sigmoid(gates[:, 0:nh])
    xg = gates[:, nh:2 * nh] + dt_ref[...]
    g = -jnp.exp(alog_ref[...]) * (jnp.maximum(xg, 0.0) + jnp.log1p(jnp.exp(-jnp.abs(xg))))

    tri2 = tri_ref[...]
    gsum = _dot_exact_l(tri2, g)
    gcum = gsum[0:c]
    grest = gsum[c:2 * c]
    exp_g = jnp.exp(gcum)
    kb = k * beta
    vb = v * beta
    kbg = kb * exp_g

    ti = lax.broadcasted_iota(jnp.int32, (c, LANES), 0)
    si = lax.broadcasted_iota(jnp.int32, (c, LANES), 1) & (HEAD_DIM - 1)
    incl = ti >= si
    strict = ti > si
    zero = jnp.zeros((c, LANES), F32)
    eye = jnp.where(ti == si, 1.0, 0.0).astype(F32)
    same16 = (ti >> 4) == (si >> 4)

    def pmm(a, b):
        return _dot3(a, _blockdiag2(b))

    us, ws, qks = [], [], []
    for p in range(C_HEADS // 2):
        sl = slice(p * LANES, (p + 1) * LANES)
        ldiff = _dot_exact_l(tri2[0:c], jnp.where(strict, g[:, sl], zero))
        lmat = jnp.where(incl, jnp.exp(jnp.where(incl, ldiff, zero)), zero)
        amat = jnp.where(strict, _dot3(kb[:, sl], _blockdiag2(k[:, sl]), NT) * lmat, zero)
        dmat = jnp.where(same16, amat, zero)
        noff = amat - dmat
        bm = -dmat
        b2 = pmm(bm, bm)
        b4 = pmm(b2, b2)
        b8 = pmm(b4, b4)
        td = eye + bm
        td = td + pmm(td, b2)
        td = td + pmm(td, b4)
        td = td + pmm(td, b8)
        mm = pmm(td, noff)
        m2 = pmm(mm, mm)
        imm = eye - mm
        tinv = pmm(imm + pmm(imm, m2), td)
        us.append(pmm(tinv, vb[:, sl]))
        ws.append(pmm(tinv, kbg[:, sl]))
        qk = _dot(q[:, sl].astype(BF16), _blockdiag2(k[:, sl].astype(BF16)), NT)
        qks.append(jnp.where(incl, qk * lmat, zero))
    u = jnp.concatenate(us, axis=1)
    wmat = jnp.concatenate(ws, axis=1)

    s = s_ref[...]
    sb = s.astype(BF16)
    v_new = u - _dot(wmat.astype(BF16), sb)
    o = _dot((q * exp_g).astype(BF16), sb)
    vnb = v_new.astype(BF16)
    o = o + jnp.concatenate(
        [_dot(qks[p].astype(BF16), _blockdiag2(vnb[:, p * LANES:(p + 1) * LANES]))
         for p in range(C_HEADS // 2)], axis=1)
    k_dec = (k * jnp.exp(grest)).astype(BF16)
    upd = _dot(k_dec, vnb, TN)
    ri = lax.broadcasted_iota(jnp.int32, (nh, nh), 0) // HEAD_DIM
    ci = lax.broadcasted_iota(jnp.int32, (nh, nh), 1) // HEAD_DIM
    s_ref[...] = s * jnp.exp(gcum[c - 1:c, :]) + jnp.where(ri == ci, upd, jnp.zeros_like(upd))

    ss = _group_sumsq(o, bd)
    yo = o * lax.rsqrt(ss * (1.0 / HEAD_DIM) + NORM_EPS) * gain_ref[...]
    o_ref[...] = (yo * _silu(cg_ref[...])).astype(o_ref.dtype)


def _gdn(proj, conv_w, a_log, dt_bias, gain, bd256, batch, seq):
    n = proj.shape[0]
    nc = seq // CHUNK
    nh = C_HEADS * HEAD_DIM
    tri2, ex = _gdn_consts()
    col0 = ((A_Q_HEADS + 2 * A_KV_HEADS) * HEAD_DIM + 4 * nh) // nh
    row = lambda b, i: b * nc + i
    const = lambda shape: pl.BlockSpec(shape, lambda b, i: (0, 0))
    rep = lambda a: jnp.repeat(a.astype(F32), HEAD_DIM).reshape(1, nh)
    return pl.pallas_call(
        _gdn_kernel,
        grid=(batch, nc),
        in_specs=[pl.BlockSpec((CHUNK, nh), lambda b, i, k=k: (row(b, i), col0 + k)) for k in range(4)]
        + [pl.BlockSpec((CHUNK, LANES), lambda b, i: (row(b, i), GATE_COL // LANES)),
           const((CONV_WIDTH, 3 * nh)), const((1, nh)), const((1, nh)), const((1, nh)),
           const(tri2.shape), const(ex.shape), const((nh, nh))],
        out_specs=pl.BlockSpec((CHUNK, nh), lambda b, i: (row(b, i), 0)),
        out_shape=jax.ShapeDtypeStruct((n, nh), BF16),
        scratch_shapes=[pltpu.VMEM((CHUNK + SUBLANES, 3 * nh), F32), pltpu.VMEM((nh, nh), F32)],
        compiler_params=pltpu.CompilerParams(dimension_semantics=("arbitrary", "arbitrary")),
        name="gated_deltanet",
    )(proj, proj, proj, proj, proj, conv_w.astype(F32), rep(a_log), rep(dt_bias),
      jnp.tile(gain.astype(F32), C_HEADS).reshape(1, nh), tri2, ex, bd256)


def _outproj_kernel(a_ref, b_ref, c_ref, x_ref, gt_ref, g_ref, sc_ref, sh_ref, w_ref, x1_ref, h_ref):
    wa = A_Q_HEADS * HEAD_DIM
    wb = wa + B_HEADS * HEAD_DIM
    y = (_dot(a_ref[...], w_ref[0:wa, :]) + _dot(b_ref[...], w_ref[wa:wb, :])
         + _dot(c_ref[...], w_ref[wb:D_MIX, :]))
    x1 = x_ref[...] + gt_ref[...] * y
    x1_ref[...] = x1
    h_ref[...] = _modulated_norm(x1, g_ref[...], sc_ref[...], sh_ref[...]).astype(h_ref.dtype)


def _outproj(out_a, out_b, out_c, x2, mod5, gain, w_out, layer, seq):
    n = x2.shape[0]
    tm = min(512, seq)
    tpb = seq // tm
    tile = lambda width: pl.BlockSpec((tm, width), lambda i: (i, 0))
    return pl.pallas_call(
        _outproj_kernel,
        grid=(n // tm,),
        in_specs=[tile(out_a.shape[1]), tile(out_b.shape[1]), tile(out_c.shape[1]), tile(D_MODEL),
                  _mod_spec(layer, 2, tpb), pl.BlockSpec((1, D_MODEL), lambda i: (0, 0)),
                  _mod_spec(layer, 4, tpb), _mod_spec(layer, 3, tpb),
                  pl.BlockSpec((D_MIX, D_MODEL), lambda i: (0, 0))],
        out_specs=[tile(D_MODEL), tile(D_MODEL)],
        out_shape=[jax.ShapeDtypeStruct((n, D_MODEL), F32), jax.ShapeDtypeStruct((n, D_MODEL), BF16)],
        compiler_params=pltpu.CompilerParams(
            dimension_semantics=("arbitrary",), vmem_limit_bytes=VMEM_LIMIT),
        name="outproj",
    )(out_a, out_b, out_c, x2, mod5, gain, mod5, mod5, w_out)


def _ffn_kernel(h_ref, x_ref, gt_ref, wg_ref, wu_ref, wd_ref, o_ref, acc_ref):
    j = pl.program_id(1)
    h = h_ref[...]
    act = (_silu(_dot(h, wg_ref[...])) * _dot(h, wu_ref[...])).astype(BF16)
    part = _dot(act, wd_ref[...])

    @pl.when(j == 0)
    def _():
        acc_ref[...] = part

    @pl.when(j > 0)
    def _():
        acc_ref[...] += part

    @pl.when(j == pl.num_programs(1) - 1)
    def _():
        o_ref[...] = x_ref[...] + gt_ref[...] * acc_ref[...]


def _ffn(h2, x1, mod5, w_gate, w_up, w_down, layer, seq):
    n = x1.shape[0]
    tm = min(512, seq)
    tpb = seq // tm
    tf = D_FF // 2
    return pl.pallas_call(
        _ffn_kernel,
        grid=(n // tm, D_FF // tf),
        in_specs=[
            pl.BlockSpec((tm, D_MODEL), lambda i, j: (i, 0)),
            pl.BlockSpec((tm, D_MODEL), lambda i, j: (i, 0)),
            pl.BlockSpec((None, None, None, 1, D_MODEL), lambda i, j: (layer, 5, i // tpb, 0, 0)),
            pl.BlockSpec((D_MODEL, tf), lambda i, j: (0, j)),
            pl.BlockSpec((D_MODEL, tf), lambda i, j: (0, j)),
            pl.BlockSpec((tf, D_MODEL), lambda i, j: (j, 0)),
        ],
        out_specs=pl.BlockSpec((tm, D_MODEL), lambda i, j: (i, 0)),
        out_shape=jax.ShapeDtypeStruct((n, D_MODEL), F32),
        scratch_shapes=[pltpu.VMEM((tm, D_MODEL), F32)],
        compiler_params=pltpu.CompilerParams(
            dimension_semantics=("arbitrary", "arbitrary"), vmem_limit_bytes=VMEM_LIMIT),
        name="ffn",
    )(h2, x1, mod5, w_gate, w_up, w_down)


def _blockdiag_ones(size):
    i = np.arange(size) // HEAD_DIM
    return jnp.asarray((i[:, None] == i[None, :]).astype(np.float32), BF16)


def kernel(x, c, positions, ada_w, ada_b, norm_mix, w_in, attn_q_norm, attn_k_norm, attn_sinks,
           hgrn_lb_logits, hgrn_out_norm, gdn_conv_w, gdn_a_log, gdn_dt_bias, gdn_out_norm, w_out,
           norm_ffn, w_gate, w_up, w_down):
    batch, seq, _ = x.shape
    depth = ada_w.shape[0]
    n = batch * seq
    x2 = x.reshape(n, D_MODEL).astype(F32)

    mod = _modulation(c.astype(F32), ada_w.astype(F32), ada_b.astype(F32))
    mod5 = mod.reshape(depth, 6, batch, 1, D_MODEL)
    cos_t, sin_t = _rope_tables(positions)
    bd128 = _blockdiag_ones(LANES)
    bd256 = _blockdiag_ones(B_HEADS * HEAD_DIM)

    for l in range(depth):
        w_pad = jnp.pad(w_in[l].astype(BF16), ((0, 0), (0, D_IN_PAD - D_IN)))
        proj = _inproj(x2, mod5, norm_mix[l].astype(F32).reshape(1, D_MODEL), w_pad, l, seq)
        out_a = _attention(proj, cos_t, sin_t, attn_q_norm[l], attn_k_norm[l], attn_sinks[l], bd128,
                           batch, seq)
        out_b = _hgrn(proj, hgrn_lb_logits, hgrn_out_norm[l], bd256, l, batch, seq)
        out_c = _gdn(proj, gdn_conv_w[l], gdn_a_log[l], gdn_dt_bias[l], gdn_out_norm[l], bd256,
                     batch, seq)
        x1, h2 = _outproj(out_a, out_b, out_c, x2, mod5, norm_ffn[l].astype(F32).reshape(1, D_MODEL),
                          w_out[l].astype(BF16), l, seq)
        x2 = _ffn(h2, x1, mod5, w_gate[l].astype(BF16), w_up[l].astype(BF16), w_down[l].astype(BF16),
                  l, seq)
    return x2.reshape(batch, seq, D_MODEL).astype(x.dtype)
```

```python
import functools
import math

import numpy as np
import jax
import jax.numpy as jnp
from jax import lax
from jax.experimental import pallas as pl
from jax.experimental.pallas import tpu as pltpu

F32 = jnp.float32
BF16 = jnp.bfloat16

D_MODEL = 1024
HEAD_DIM = 64
A_Q_HEADS = 8
A_KV_HEADS = 2
WINDOW = 128
ROPE_DIM = HEAD_DIM // 4
ROPE_THETA = 500000.0
B_HEADS = 4
C_HEADS = 4
CONV_WIDTH = 4
CHUNK = 64
D_FF = 2816
D_MIX = 1024
D_IN = 2824
NORM_EPS = 1e-6

LANES = 128
SUBLANES = 8
D_IN_PAD = 2944
GATE_COL = 2816
VMEM_LIMIT = 56 * 1024 * 1024

NN = (((1,), (0,)), ((), ()))
NT = (((1,), (1,)), ((), ()))
TN = (((0,), (0,)), ((), ()))
BNN = (((2,), (1,)), ((0,), (0,)))
BNT = (((2,), (2,)), ((0,), (0,)))


def _dot(a, b, dims=NN):
    return lax.dot_general(a, b, dims, preferred_element_type=F32)


def _split(x, n):
    parts = []
    r = x
    for i in range(n):
        p = r.astype(BF16)
        parts.append(p)
        if i + 1 < n:
            r = r - p.astype(F32)
    return parts


def _dot_exact_l(m, x, n=3):
    out = None
    for p in _split(x, n):
        t = _dot(m, p)
        out = t if out is None else out + t
    return out


def _dot_exact_r(x, m, n=3):
    out = None
    for p in _split(x, n):
        t = _dot(p, m)
        out = t if out is None else out + t
    return out


def _dot3(a, b, dims=NN):
    a1, a2 = _split(a, 2)
    b1, b2 = _split(b, 2)
    return _dot(a1, b1, dims) + _dot(a1, b2, dims) + _dot(a2, b1, dims)


def _silu(x):
    return x * jax.nn.sigmoid(x)


def _lane_lo(shape):
    return (lax.broadcasted_iota(jnp.int32, shape, len(shape) - 1) & (LANES - 1)) < HEAD_DIM


def _blockdiag2(x):
    lo = _lane_lo(x.shape)
    zero = jnp.zeros_like(x)
    return jnp.concatenate([jnp.where(lo, x, zero), jnp.where(lo, zero, x)], axis=x.ndim - 2)


def _group_sumsq(x, bd):
    return _dot_exact_r(x * x, bd, n=2)


def _mod_kernel(c_ref, w_ref, b_ref, o_ref):
    c = c_ref[...]
    o_ref[...] = _dot3(_silu(c), w_ref[...]) + b_ref[...]


def _modulation(c, ada_w, ada_b):
    depth = ada_w.shape[0]
    b = c.shape[0]
    return pl.pallas_call(
        _mod_kernel,
        grid=(depth, 6),
        in_specs=[
            pl.BlockSpec((b, D_MODEL), lambda l, k: (0, 0)),
            pl.BlockSpec((None, D_MODEL, D_MODEL), lambda l, k: (l, 0, k)),
            pl.BlockSpec((None, None, 1, D_MODEL), lambda l, k: (l, k, 0, 0)),
        ],
        out_specs=pl.BlockSpec((None, None, b, D_MODEL), lambda l, k: (l, k, 0, 0)),
        out_shape=jax.ShapeDtypeStruct((depth, 6, b, D_MODEL), F32),
        compiler_params=pltpu.CompilerParams(
            dimension_semantics=("arbitrary", "arbitrary"), vmem_limit_bytes=VMEM_LIMIT),
        name="modulation",
    )(c, ada_w, ada_b.reshape(depth, 6, 1, D_MODEL))


def _rope_kernel(pos_ref, inv_ref, sgn_ref, cos_ref, sin_ref):
    ang = pos_ref[...].astype(F32) * inv_ref[...]
    cos_ref[...] = jnp.cos(ang)
    sin_ref[...] = jnp.sin(ang) * sgn_ref[...]


def _rope_tables(positions):
    n = positions.size
    tm = min(2048, n)
    half = ROPE_DIM // 2
    inv_freq = ROPE_THETA ** (-jnp.arange(half, dtype=F32) * 2.0 / ROPE_DIM)
    lane = np.arange(LANES) % HEAD_DIM
    inv_lane = jnp.where(lane < ROPE_DIM, inv_freq[lane % half], 0.0).reshape(1, LANES).astype(F32)
    sgn_lane = jnp.asarray(np.where(lane < half, -1.0, np.where(lane < ROPE_DIM, 1.0, 0.0)),
                           F32).reshape(1, LANES)
    return pl.pallas_call(
        _rope_kernel,
        grid=(n // tm,),
        in_specs=[
            pl.BlockSpec((tm, 1), lambda i: (i, 0)),
            pl.BlockSpec((1, LANES), lambda i: (0, 0)),
            pl.BlockSpec((1, LANES), lambda i: (0, 0)),
        ],
        out_specs=[pl.BlockSpec((tm, LANES), lambda i: (i, 0))] * 2,
        out_shape=[jax.ShapeDtypeStruct((n, LANES), F32)] * 2,
        compiler_params=pltpu.CompilerParams(dimension_semantics=("arbitrary",)),
        name="rope_tables",
    )(positions.reshape(n, 1), inv_lane, sgn_lane)


def _modulated_norm(x, gain, scale, shift):
    ms = jnp.mean(x * x, axis=-1, keepdims=True)
    return x * lax.rsqrt(ms + NORM_EPS) * gain * (1.0 + scale) + shift


def _inproj_kernel(x_ref, g_ref, sc_ref, sh_ref, w_ref, o_ref):
    h = _modulated_norm(x_ref[...], g_ref[...], sc_ref[...], sh_ref[...])
    o_ref[...] = _dot(h.astype(BF16), w_ref[...])


def _mod_spec(layer, which, tiles_per_batch):
    return pl.BlockSpec((None, None, None, 1, D_MODEL),
                        lambda i: (layer, which, i // tiles_per_batch, 0, 0))


def _inproj(x2, mod5, gain, w_pad, layer, seq):
    n = x2.shape[0]
    tm = min(512, seq)
    tpb = seq // tm
    return pl.pallas_call(
        _inproj_kernel,
        grid=(n // tm,),
        in_specs=[
            pl.BlockSpec((tm, D_MODEL), lambda i: (i, 0)),
            pl.BlockSpec((1, D_MODEL), lambda i: (0, 0)),
            _mod_spec(layer, 1, tpb),
            _mod_spec(layer, 0, tpb),
            pl.BlockSpec((D_MODEL, D_IN_PAD), lambda i: (0, 0)),
        ],
        out_specs=pl.BlockSpec((tm, D_IN_PAD), lambda i: (i, 0)),
        out_shape=jax.ShapeDtypeStruct((n, D_IN_PAD), F32),
        compiler_params=pltpu.CompilerParams(
            dimension_semantics=("arbitrary",), vmem_limit_bytes=VMEM_LIMIT),
        name="inproj",
    )(x2, gain, mod5, mod5, w_pad)


def _attn_kernel(sink_ref, q_ref, k_ref, v_ref, cos_ref, sin_ref, qg_ref, kg_ref, bd_ref,
                 o_ref, kvar_ref, vvar_ref):
    n = pl.program_id(1)
    w = WINDOW

    @pl.when(n == 0)
    def _():
        kvar_ref[:, 0:w, :] = jnp.zeros((4, w, LANES), BF16)
        vvar_ref[:, 0:w, :] = jnp.zeros((4, w, LANES), BF16)

    @pl.when(n > 0)
    def _():
        kvar_ref[:, 0:w, :] = kvar_ref[:, w:2 * w, :]
        vvar_ref[:, 0:w, :] = vvar_ref[:, w:2 * w, :]

    cos = cos_ref[...]
    sin = sin_ref[...]
    bd = bd_ref[...]
    lane = lax.broadcasted_iota(jnp.int32, (w, LANES), 1)
    lo = lane < HEAD_DIM
    take_next = (lane & (HEAD_DIM - 1)) < (ROPE_DIM // 2)

    def norm_rope(xp, gain):
        ss = _group_sumsq(xp, bd)
        y = xp * lax.rsqrt(ss * (1.0 / HEAD_DIM) + NORM_EPS) * gain
        other = jnp.where(take_next, pltpu.roll(y, LANES - ROPE_DIM // 2, 1),
                          pltpu.roll(y, ROPE_DIM // 2, 1))
        return y * cos + other * sin

    def variants(x):
        xs = pltpu.roll(x, HEAD_DIM, 1)
        zero = jnp.zeros_like(x)
        return [jnp.where(lo, x, zero), jnp.where(lo, zero, xs),
                jnp.where(lo, xs, zero), jnp.where(lo, zero, x)]

    kn = norm_rope(k_ref[...], kg_ref[...])
    for i, t in enumerate(variants(kn)):
        kvar_ref[i, w:2 * w, :] = t.astype(BF16)
    for i, t in enumerate(variants(v_ref[...])):
        vvar_ref[i, w:2 * w, :] = t.astype(BF16)

    qi = lax.broadcasted_iota(jnp.int32, (w, 2 * w), 0)
    kj = lax.broadcasted_iota(jnp.int32, (w, 2 * w), 1)
    delta = qi + w - kj
    valid = (delta >= 0) & (delta < w) & ((n * w + kj - w) >= 0)

    for p in range(A_Q_HEADS // 2):
        g = (2 * p) // (A_Q_HEADS // A_KV_HEADS)
        qn = norm_rope(q_ref[:, p * LANES:(p + 1) * LANES], qg_ref[...]) * (HEAD_DIM ** -0.5)
        qb = qn.astype(BF16)
        out = None
        for par in range(2):
            sink = sink_ref[2 * p + par]
            s = _dot(qb, kvar_ref[2 * g + par], NT)
            s = jnp.where(valid, s, -jnp.inf)
            m = jnp.maximum(jnp.max(s, axis=-1, keepdims=True), sink)
            e = jnp.exp(s - m)
            den = jnp.sum(e, axis=-1, keepdims=True) + jnp.exp(sink - m)
            o = _dot(e.astype(BF16), vvar_ref[2 * g + par]) / den
            out = o if out is None else out + o
        o_ref[:, p * LANES:(p + 1) * LANES] = out.astype(o_ref.dtype)


def _attention(proj, cos_t, sin_t, q_norm, k_norm, sinks, bd128, batch, seq):
    n = proj.shape[0]
    nb = seq // WINDOW
    qw = A_Q_HEADS * HEAD_DIM
    row = lambda b, i: b * nb + i
    qg = jnp.tile(q_norm.astype(F32), 2).reshape(1, LANES)
    kg = jnp.tile(k_norm.astype(F32), 2).reshape(1, LANES)
    return pl.pallas_call(
        _attn_kernel,
        grid=(batch, nb),
        in_specs=[
            pl.BlockSpec(memory_space=pltpu.SMEM),
            pl.BlockSpec((WINDOW, qw), lambda b, i: (row(b, i), 0)),
            pl.BlockSpec((WINDOW, LANES), lambda b, i: (row(b, i), qw // LANES)),
            pl.BlockSpec((WINDOW, LANES), lambda b, i: (row(b, i), qw // LANES + 1)),
            pl.BlockSpec((WINDOW, LANES), lambda b, i: (row(b, i), 0)),
            pl.BlockSpec((WINDOW, LANES), lambda b, i: (row(b, i), 0)),
            pl.BlockSpec((1, LANES), lambda b, i: (0, 0)),
            pl.BlockSpec((1, LANES), lambda b, i: (0, 0)),
            pl.BlockSpec((LANES, LANES), lambda b, i: (0, 0)),
        ],
        out_specs=pl.BlockSpec((WINDOW, qw), lambda b, i: (row(b, i), 0)),
        out_shape=jax.ShapeDtypeStruct((n, qw), BF16),
        scratch_shapes=[pltpu.VMEM((4, 2 * WINDOW, LANES), BF16),
                        pltpu.VMEM((4, 2 * WINDOW, LANES), BF16)],
        compiler_params=pltpu.CompilerParams(dimension_semantics=("arbitrary", "arbitrary")),
        name="attention",
    )(sinks.astype(F32), proj, proj, proj, cos_t, sin_t, qg, kg, bd128)


def _hgrn_consts():
    c = CHUNK
    t = np.arange(c)[:, None]
    r = np.arange(c)[None, :]
    mats = [(r <= t).astype(np.float32), (r > t).astype(np.float32)]
    for l in range(6):
        ref = ((t >> (l + 1)) << (l + 1)) + (1 << l) - 1
        mats.append((r <= t).astype(np.float32) - (r <= ref).astype(np.float32))
    mall = np.concatenate(mats, axis=0)
    s = np.arange(LANES)[None, :] % c
    x = t ^ s
    lvl = np.where(t > s, np.floor(np.log2(np.maximum(x, 1))).astype(np.int32),
                   np.where(t == s, -1, -2)).astype(np.int32)
    return jnp.asarray(mall, BF16), jnp.asarray(lvl, jnp.int32)


def _head_block_mask(nh):
    ri = lax.broadcasted_iota(jnp.int32, (nh, nh), 0) // HEAD_DIM
    ci = lax.broadcasted_iota(jnp.int32, (nh, nh), 1) // HEAD_DIM
    return ri == ci


def _hgrn_kernel(q_ref, z_ref, v_ref, gt_ref, lbl_ref, gain_ref, mall_ref, lvl_ref, bd_ref,
                 o_ref, st_ref, *, layer):
    c = CHUNK
    nh = B_HEADS * HEAD_DIM
    nb = q_ref.shape[0]
    rows = nb * c

    @pl.when(pl.program_id(0) == 0)
    def _():
        st_ref[...] = jnp.zeros_like(st_ref)

    q = q_ref[...].reshape(rows, nh)
    z = z_ref[...].reshape(rows, nh)
    v = v_ref[...].reshape(rows, nh)

    lg = lbl_ref[...]
    e = jnp.exp(lg - jnp.max(lg, axis=0, keepdims=True))
    sm = e / jnp.sum(e, axis=0, keepdims=True)
    lb = jnp.zeros((1, nh), F32)
    for j in range(1, layer + 1):
        lb = lb + sm[j:j + 1, :]

    a = jnp.log(lb)
    b = jnp.log1p(-lb) + jnp.minimum(z, 0.0) - jnp.log1p(jnp.exp(-jnp.abs(z)))
    lf = jnp.maximum(a, b) + jnp.log1p(jnp.exp(-jnp.abs(a - b)))
    kk = (1.0 - lb) * jax.nn.sigmoid(-z)

    mall = jnp.broadcast_to(mall_ref[...][None], (nb,) + mall_ref.shape)
    dall = None
    for piece in _split(lf.reshape(nb, c, nh), 3):
        t = _dot(mall, piece, BNN)
        dall = t if dall is None else dall + t
    cum = dall[:, 0:c]
    rest = dall[:, c:2 * c]

    q3 = q.reshape(nb, c, nh)
    k3 = kk.reshape(nb, c, nh)
    v3 = v.reshape(nb, c, nh)
    lvl = lvl_ref[...][None]
    trow = lax.broadcasted_iota(jnp.int32, (1, c, LANES), 1)
    zero = jnp.zeros((nb, c, LANES), F32)
    outs = []
    for p in range(B_HEADS // 2):
        sl = slice(p * LANES, (p + 1) * LANES)
        qp, kp, vp = q3[:, :, sl], k3[:, :, sl], v3[:, :, sl]
        pm = jnp.where(lvl == -1, _dot(qp.astype(BF16), _blockdiag2(kp.astype(BF16)), BNT), zero)
        for l in range(6):
            dec = jnp.exp(-jnp.abs(dall[:, (2 + l) * c:(3 + l) * c, sl]))
            is_q = ((trow >> l) & 1) == 1
            xl = (jnp.where(is_q, qp, kp) * dec).astype(BF16)
            pm = pm + jnp.where(lvl == l, _dot(xl, _blockdiag2(xl), BNT), zero)
        outs.append(_dot(pm.astype(BF16), _blockdiag2(vp.astype(BF16)), BNN))
    o_intra = jnp.concatenate(outs, axis=2)

    q_dec = (q3 * jnp.exp(cum)).astype(BF16)
    k_dec = (k3 * jnp.exp(rest)).astype(BF16)
    vb = v3.astype(BF16)
    dec_last = jnp.exp(cum[:, c - 1:c, :])
    blk = _head_block_mask(nh)
    o_inter = []
    for i in range(nb):
        st = st_ref[i]
        o_inter.append(_dot(q_dec[i], st.astype(BF16), NT))
        upd = _dot(vb[i], k_dec[i], TN)
        st_ref[i] = st * dec_last[i] + jnp.where(blk, upd, jnp.zeros_like(upd))
    o = (o_intra + jnp.stack(o_inter, axis=0)).reshape(rows, nh)

    ss = _dot(jnp.square(o).astype(BF16), bd_ref[...])
    y = o * lax.rsqrt(ss * (1.0 / HEAD_DIM) + NORM_EPS) * gain_ref[...]
    o_ref[...] = (y * _silu(gt_ref[...].reshape(rows, nh))).reshape(nb, c, nh).astype(o_ref.dtype)


def _hgrn(proj3, lb_logits, gain, bd256, layer):
    batch, seq, _ = proj3.shape
    nc = seq // CHUNK
    nh = B_HEADS * HEAD_DIM
    depth = lb_logits.shape[0]
    mall, lvl = _hgrn_consts()
    col0 = (A_Q_HEADS + 2 * A_KV_HEADS) * HEAD_DIM // nh
    const = lambda shape: pl.BlockSpec(shape, lambda i: (0, 0))
    return pl.pallas_call(
        functools.partial(_hgrn_kernel, layer=layer),
        grid=(nc,),
        in_specs=[pl.BlockSpec((batch, CHUNK, nh), lambda i, k=k: (0, i, col0 + k)) for k in range(4)]
        + [const((depth, nh)), const((1, nh)), const(mall.shape), const(lvl.shape), const((nh, nh))],
        out_specs=pl.BlockSpec((batch, CHUNK, nh), lambda i: (0, i, 0)),
        out_shape=jax.ShapeDtypeStruct((batch, seq, nh), BF16),
        scratch_shapes=[pltpu.VMEM((batch, nh, nh), F32)],
        compiler_params=pltpu.CompilerParams(
            dimension_semantics=("arbitrary",), vmem_limit_bytes=VMEM_LIMIT),
        name="hgrn2",
    )(proj3, proj3, proj3, proj3, lb_logits.astype(F32),
      jnp.tile(gain.astype(F32), B_HEADS).reshape(1, nh), mall, lvl, bd256)


def _gdn_consts():
    c = CHUNK
    t = np.arange(c)[:, None]
    r = np.arange(c)[None, :]
    tri2 = np.concatenate([(r <= t), (r > t)], axis=0).astype(np.float32)
    nh = C_HEADS * HEAD_DIM
    ex = np.zeros((LANES, 2 * nh), np.float32)
    for h in range(C_HEADS):
        ex[h, h * HEAD_DIM:(h + 1) * HEAD_DIM] = 1.0
        ex[C_HEADS + h, nh + h * HEAD_DIM:nh + (h + 1) * HEAD_DIM] = 1.0
    return jnp.asarray(tri2, BF16), jnp.asarray(ex, BF16)


def _gdn_kernel(cq_ref, ck_ref, cv_ref, cg_ref, gate_ref, cw_ref, alog_ref, dt_ref, gain_ref,
                tri_ref, ex_ref, bd_ref, o_ref, xbuf_ref, s_ref):
    c = CHUNK
    nh = C_HEADS * HEAD_DIM
    nb = cq_ref.shape[0]
    rows = nb * c
    tail = SUBLANES

    @pl.when(pl.program_id(0) == 0)
    def _():
        s_ref[...] = jnp.zeros_like(s_ref)
        xbuf_ref[:, 0:tail, :] = jnp.zeros((nb, tail, 3 * nh), F32)

    xbuf_ref[:, tail:tail + c, 0:nh] = cq_ref[...]
    xbuf_ref[:, tail:tail + c, nh:2 * nh] = ck_ref[...]
    xbuf_ref[:, tail:tail + c, 2 * nh:3 * nh] = cv_ref[...]
    w = cw_ref[...]
    y = None
    for j in range(CONV_WIDTH):
        off = tail - (CONV_WIDTH - 1) + j
        t = xbuf_ref[:, off:off + c, :] * w[j:j + 1, :][None]
        y = t if y is None else y + t
    xbuf_ref[:, 0:tail, :] = xbuf_ref[:, c:c + tail, :]
    y = _silu(y).reshape(rows, 3 * nh)
    q, k, v = y[:, 0:nh], y[:, nh:2 * nh], y[:, 2 * nh:3 * nh]

    bd = bd_ref[...]
    q = q * lax.rsqrt(_dot(jnp.square(q).astype(BF16), bd) + NORM_EPS) * (HEAD_DIM ** -0.5)
    k = k * lax.rsqrt(_dot(jnp.square(k).astype(BF16), bd) + NORM_EPS)

    gates = _dot_exact_r(gate_ref[...].reshape(rows, LANES), ex_ref[...])
    beta = jax.nn.sigmoid(gates[:, 0:nh])
    xg = gates[:, nh:2 * nh] + dt_ref[...]
    g = -jnp.exp(alog_ref[...]) * (jnp.maximum(xg, 0.0) + jnp.log1p(jnp.exp(-jnp.abs(xg))))

    g3 = g.reshape(nb, c, nh)
    tri2 = jnp.broadcast_to(tri_ref[...][None], (nb, 2 * c, c))
    gsum = None
    for piece in _split(g3, 3):
        t = _dot(tri2, piece, BNN)
        gsum = t if gsum is None else gsum + t
    gcum = gsum[:, 0:c]
    grest = gsum[:, c:2 * c]
    exp_g = jnp.exp(gcum)
    q3 = q.reshape(nb, c, nh)
    k3 = k.reshape(nb, c, nh)
    beta3 = beta.reshape(nb, c, nh)
    kb = k3 * beta3
    vb = v.reshape(nb, c, nh) * beta3
    kbg = kb * exp_g

    ti = lax.broadcasted_iota(jnp.int32, (1, c, LANES), 1)
    si = lax.broadcasted_iota(jnp.int32, (1, c, LANES), 2) & (HEAD_DIM - 1)
    incl = ti >= si
    strict = ti > si
    zero = jnp.zeros((nb, c, LANES), F32)
    eye = jnp.where(ti == si, 1.0, 0.0).astype(F32)
    same16 = (ti >> 4) == (si >> 4)

    def pmm(a, b):
        return _dot(a.astype(BF16), _blockdiag2(b.astype(BF16)), BNN)

    us, ws, qks = [], [], []
    for p in range(C_HEADS // 2):
        sl = slice(p * LANES, (p + 1) * LANES)
        ldiff = None
        for piece in _split(jnp.where(strict, g3[:, :, sl], zero), 3):
            t = _dot(tri2[:, 0:c], piece, BNN)
            ldiff = t if ldiff is None else ldiff + t
        lmat = jnp.where(incl, jnp.exp(jnp.where(incl, ldiff, zero)), zero)
        kbd = _blockdiag2(k3[:, :, sl].astype(BF16))
        amat = jnp.where(strict, _dot(kb[:, :, sl].astype(BF16), kbd, BNT) * lmat, zero)
        dmat = jnp.where(same16, amat, zero)
        noff = amat - dmat
        bm = -dmat
        b2 = pmm(bm, bm)
        b4 = pmm(b2, b2)
        b8 = pmm(b4, b4)
        td = eye + bm
        td = td + pmm(td, b2)
        td = td + pmm(td, b4)
        td = td + pmm(td, b8)
        mm = pmm(td, noff)
        m2 = pmm(mm, mm)
        imm = eye - mm
        tinv = pmm(imm + pmm(imm, m2), td).astype(BF16)
        us.append(_dot(tinv, _blockdiag2(vb[:, :, sl].astype(BF16)), BNN))
        ws.append(_dot(tinv, _blockdiag2(kbg[:, :, sl].astype(BF16)), BNN))
        qk = _dot(q3[:, :, sl].astype(BF16), kbd, BNT)
        qks.append(jnp.where(incl, qk * lmat, zero).astype(BF16))
    u = jnp.concatenate(us, axis=2)
    wmat = jnp.concatenate(ws, axis=2).astype(BF16)
    q_dec = (q3 * exp_g).astype(BF16)
    k_dec = (k3 * jnp.exp(grest)).astype(BF16)
    dec_last = jnp.exp(gcum[:, c - 1:c, :])
    blk = _head_block_mask(nh)
    outs = []
    for i in range(nb):
        s = s_ref[i]
        sb = s.astype(BF16)
        v_new = (u[i] - _dot(wmat[i], sb)).astype(BF16)
        o = _dot(q_dec[i], sb)
        o = o + jnp.concatenate(
            [_dot(qks[p][i], _blockdiag2(v_new[:, p * LANES:(p + 1) * LANES]))
             for p in range(C_HEADS // 2)], axis=1)
        outs.append(o)
        upd = _dot(k_dec[i], v_new, TN)
        s_ref[i] = s * dec_last[i] + jnp.where(blk, upd, jnp.zeros_like(upd))
    o = jnp.stack(outs, axis=0).reshape(rows, nh)

    ss = _dot(jnp.square(o).astype(BF16), bd)
    yo = o * lax.rsqrt(ss * (1.0 / HEAD_DIM) + NORM_EPS) * gain_ref[...]
    o_ref[...] = (yo * _silu(cg_ref[...].reshape(rows, nh))).reshape(nb, c, nh).astype(o_ref.dtype)


def _gdn(proj3, conv_w, a_log, dt_bias, gain, bd256):
    batch, seq, _ = proj3.shape
    nc = seq // CHUNK
    nh = C_HEADS * HEAD_DIM
    tri2, ex = _gdn_consts()
    col0 = ((A_Q_HEADS + 2 * A_KV_HEADS) * HEAD_DIM + 4 * nh) // nh
    const = lambda shape: pl.BlockSpec(shape, lambda i: (0, 0))
    rep = lambda a: jnp.repeat(a.astype(F32), HEAD_DIM).reshape(1, nh)
    return pl.pallas_call(
        _gdn_kernel,
        grid=(nc,),
        in_specs=[pl.BlockSpec((batch, CHUNK, nh), lambda i, k=k: (0, i, col0 + k)) for k in range(4)]
        + [pl.BlockSpec((batch, CHUNK, LANES), lambda i: (0, i, GATE_COL // LANES)),
           const((CONV_WIDTH, 3 * nh)), const((1, nh)), const((1, nh)), const((1, nh)),
           const(tri2.shape), const(ex.shape), const((nh, nh))],
        out_specs=pl.BlockSpec((batch, CHUNK, nh), lambda i: (0, i, 0)),
        out_shape=jax.ShapeDtypeStruct((batch, seq, nh), BF16),
        scratch_shapes=[pltpu.VMEM((batch, CHUNK + SUBLANES, 3 * nh), F32),
                        pltpu.VMEM((batch, nh, nh), F32)],
        compiler_params=pltpu.CompilerParams(
            dimension_semantics=("arbitrary",), vmem_limit_bytes=VMEM_LIMIT),
        name="gated_deltanet",
    )(proj3, proj3, proj3, proj3, proj3, conv_w.astype(F32), rep(a_log), rep(dt_bias),
      jnp.tile(gain.astype(F32), C_HEADS).reshape(1, nh), tri2, ex, bd256)


def _outproj_kernel(a_ref, b_ref, c_ref, x_ref, gt_ref, g_ref, sc_ref, sh_ref, w_ref, x1_ref, h_ref):
    wa = A_Q_HEADS * HEAD_DIM
    wb = wa + B_HEADS * HEAD_DIM
    y = (_dot(a_ref[...], w_ref[0:wa, :]) + _dot(b_ref[...], w_ref[wa:wb, :])
         + _dot(c_ref[...], w_ref[wb:D_MIX, :]))
    x1 = x_ref[...] + gt_ref[...] * y
    x1_ref[...] = x1
    h_ref[...] = _modulated_norm(x1, g_ref[...], sc_ref[...], sh_ref[...]).astype(h_ref.dtype)


def _outproj(out_a, out_b, out_c, x2, mod5, gain, w_out, layer, seq):
    n = x2.shape[0]
    tm = min(512, seq)
    tpb = seq // tm
    tile = lambda width: pl.BlockSpec((tm, width), lambda i: (i, 0))
    return pl.pallas_call(
        _outproj_kernel,
        grid=(n // tm,),
        in_specs=[tile(out_a.shape[1]), tile(out_b.shape[1]), tile(out_c.shape[1]), tile(D_MODEL),
                  _mod_spec(layer, 2, tpb), pl.BlockSpec((1, D_MODEL), lambda i: (0, 0)),
                  _mod_spec(layer, 4, tpb), _mod_spec(layer, 3, tpb),
                  pl.BlockSpec((D_MIX, D_MODEL), lambda i: (0, 0))],
        out_specs=[tile(D_MODEL), tile(D_MODEL)],
        out_shape=[jax.ShapeDtypeStruct((n, D_MODEL), F32), jax.ShapeDtypeStruct((n, D_MODEL), BF16)],
        compiler_params=pltpu.CompilerParams(
            dimension_semantics=("arbitrary",), vmem_limit_bytes=VMEM_LIMIT),
        name="outproj",
    )(out_a, out_b, out_c, x2, mod5, gain, mod5, mod5, w_out)


def _ffn_kernel(h_ref, x_ref, gt_ref, wg_ref, wu_ref, wd_ref, o_ref, acc_ref):
    j = pl.program_id(1)
    h = h_ref[...]
    act = (_silu(_dot(h, wg_ref[...])) * _dot(h, wu_ref[...])).astype(BF16)
    part = _dot(act, wd_ref[...])

    @pl.when(j == 0)
    def _():
        acc_ref[...] = part

    @pl.when(j > 0)
    def _():
        acc_ref[...] += part

    @pl.when(j == pl.num_programs(1) - 1)
    def _():
        o_ref[...] = x_ref[...] + gt_ref[...] * acc_ref[...]


def _ffn(h2, x1, mod5, w_gate, w_up, w_down, layer, seq):
    n = x1.shape[0]
    tm = min(512, seq)
    tpb = seq // tm
    tf = D_FF // 2
    return pl.pallas_call(
        _ffn_kernel,
        grid=(n // tm, D_FF // tf),
        in_specs=[
            pl.BlockSpec((tm, D_MODEL), lambda i, j: (i, 0)),
            pl.BlockSpec((tm, D_MODEL), lambda i, j: (i, 0)),
            pl.BlockSpec((None, None, None, 1, D_MODEL), lambda i, j: (layer, 5, i // tpb, 0, 0)),
            pl.BlockSpec((D_MODEL, tf), lambda i, j: (0, j)),
            pl.BlockSpec((D_MODEL, tf), lambda i, j: (0, j)),
            pl.BlockSpec((tf, D_MODEL), lambda i, j: (j, 0)),
        ],
        out_specs=pl.BlockSpec((tm, D_MODEL), lambda i, j: (i, 0)),
        out_shape=jax.ShapeDtypeStruct((n, D_MODEL), F32),
        scratch_shapes=[pltpu.VMEM((tm, D_MODEL), F32)],
        compiler_params=pltpu.CompilerParams(
            dimension_semantics=("arbitrary", "arbitrary"), vmem_limit_bytes=VMEM_LIMIT),
        name="ffn",
    )(h2, x1, mod5, w_gate, w_up, w_down)


def _blockdiag_ones(size):
    i = np.arange(size) // HEAD_DIM
    return jnp.asarray((i[:, None] == i[None, :]).astype(np.float32), BF16)


def kernel(x, c, positions, ada_w, ada_b, norm_mix, w_in, attn_q_norm, attn_k_norm, attn_sinks,
           hgrn_lb_logits, hgrn_out_norm, gdn_conv_w, gdn_a_log, gdn_dt_bias, gdn_out_norm, w_out,
           norm_ffn, w_gate, w_up, w_down):
    batch, seq, _ = x.shape
    depth = ada_w.shape[0]
    n = batch * seq
    x2 = x.reshape(n, D_MODEL).astype(F32)

    mod = _modulation(c.astype(F32), ada_w.astype(F32), ada_b.astype(F32))
    mod5 = mod.reshape(depth, 6, batch, 1, D_MODEL)
    cos_t, sin_t = _rope_tables(positions)
    bd128 = _blockdiag_ones(LANES)
    bd256 = _blockdiag_ones(B_HEADS * HEAD_DIM)

    for l in range(depth):
        w_pad = jnp.pad(w_in[l].astype(BF16), ((0, 0), (0, D_IN_PAD - D_IN)))
        proj = _inproj(x2, mod5, norm_mix[l].astype(F32).reshape(1, D_MODEL), w_pad, l, seq)
        out_a = _attention(proj, cos_t, sin_t, attn_q_norm[l], attn_k_norm[l], attn_sinks[l], bd128,
                           batch, seq)
        proj3 = proj.reshape(batch, seq, D_IN_PAD)
        out_b = _hgrn(proj3, hgrn_lb_logits, hgrn_out_norm[l], bd256, l).reshape(n, -1)
        out_c = _gdn(proj3, gdn_conv_w[l], gdn_a_log[l], gdn_dt_bias[l], gdn_out_norm[l],
                     bd256).reshape(n, -1)
        x1, h2 = _outproj(out_a, out_b, out_c, x2, mod5, norm_ffn[l].astype(F32).reshape(1, D_MODEL),
                          w_out[l].astype(BF16), l, seq)
        x2 = _ffn(h2, x1, mod5, w_gate[l].astype(BF16), w_up[l].astype(BF16), w_down[l].astype(BF16),
                  l, seq)
    return x2.reshape(batch, seq, D_MODEL).astype(x.dtype)
```

```python
import functools
import math

import numpy as np
import jax
import jax.numpy as jnp
from jax import lax
from jax.experimental import pallas as pl
from jax.experimental.pallas import tpu as pltpu

F32 = jnp.float32
BF16 = jnp.bfloat16

D_MODEL = 1024
HEAD_DIM = 64
A_Q_HEADS = 8
A_KV_HEADS = 2
WINDOW = 128
ROPE_DIM = HEAD_DIM // 4
ROPE_THETA = 500000.0
B_HEADS = 4
C_HEADS = 4
CONV_WIDTH = 4
CHUNK = 64
D_FF = 2816
D_MIX = 1024
D_IN = 2824
NORM_EPS = 1e-6
LOG2E = math.log2(math.e)

LANES = 128
SUBLANES = 8
D_IN_PAD = 2944
GATE_COL = 2816
VMEM_LIMIT = 56 * 1024 * 1024

NN = (((1,), (0,)), ((), ()))
NT = (((1,), (1,)), ((), ()))
TN = (((0,), (0,)), ((), ()))
BNN = (((2,), (1,)), ((0,), (0,)))
BNT = (((2,), (2,)), ((0,), (0,)))


def _dot(a, b, dims=NN):
    return lax.dot_general(a, b, dims, preferred_element_type=F32)


def _split(x, n):
    parts = []
    r = x
    for i in range(n):
        p = r.astype(BF16)
        parts.append(p)
        if i + 1 < n:
            r = r - p.astype(F32)
    return parts


def _dot_exact_l(m, x, n=3):
    out = None
    for p in _split(x, n):
        t = _dot(m, p)
        out = t if out is None else out + t
    return out


def _dot_exact_r(x, m, n=3):
    out = None
    for p in _split(x, n):
        t = _dot(p, m)
        out = t if out is None else out + t
    return out


def _dot3(a, b, dims=NN):
    a1, a2 = _split(a, 2)
    b1, b2 = _split(b, 2)
    return _dot(a1, b1, dims) + _dot(a1, b2, dims) + _dot(a2, b1, dims)


def _silu(x):
    return x * jax.nn.sigmoid(x)


def _lane_lo(shape):
    return (lax.broadcasted_iota(jnp.int32, shape, len(shape) - 1) & (LANES - 1)) < HEAD_DIM


def _blockdiag2(x):
    lo = _lane_lo(x.shape)
    zero = jnp.zeros_like(x)
    return jnp.concatenate([jnp.where(lo, x, zero), jnp.where(lo, zero, x)], axis=x.ndim - 2)


def _group_sumsq(x, bd):
    return _dot_exact_r(x * x, bd, n=2)


def _mod_kernel(c_ref, w_ref, b_ref, o_ref):
    c = c_ref[...]
    o_ref[...] = _dot3(_silu(c), w_ref[...]) + b_ref[...]


def _modulation(c, ada_w, ada_b):
    depth = ada_w.shape[0]
    b = c.shape[0]
    return pl.pallas_call(
        _mod_kernel,
        grid=(depth, 6),
        in_specs=[
            pl.BlockSpec((b, D_MODEL), lambda l, k: (0, 0)),
            pl.BlockSpec((None, D_MODEL, D_MODEL), lambda l, k: (l, 0, k)),
            pl.BlockSpec((None, None, 1, D_MODEL), lambda l, k: (l, k, 0, 0)),
        ],
        out_specs=pl.BlockSpec((None, None, b, D_MODEL), lambda l, k: (l, k, 0, 0)),
        out_shape=jax.ShapeDtypeStruct((depth, 6, b, D_MODEL), F32),
        compiler_params=pltpu.CompilerParams(
            dimension_semantics=("arbitrary", "arbitrary"), vmem_limit_bytes=VMEM_LIMIT),
        name="modulation",
    )(c, ada_w, ada_b.reshape(depth, 6, 1, D_MODEL))


def _rope_kernel(pos_ref, inv_ref, sgn_ref, cos_ref, sin_ref):
    ang = pos_ref[...].astype(F32) * inv_ref[...]
    cos_ref[...] = jnp.cos(ang)
    sin_ref[...] = jnp.sin(ang) * sgn_ref[...]


def _rope_tables(positions):
    n = positions.size
    tm = min(2048, n)
    half = ROPE_DIM // 2
    inv_freq = ROPE_THETA ** (-jnp.arange(half, dtype=F32) * 2.0 / ROPE_DIM)
    lane = np.arange(LANES) % HEAD_DIM
    inv_lane = jnp.where(lane < ROPE_DIM, inv_freq[lane % half], 0.0).reshape(1, LANES).astype(F32)
    sgn_lane = jnp.asarray(np.where(lane < half, -1.0, np.where(lane < ROPE_DIM, 1.0, 0.0)),
                           F32).reshape(1, LANES)
    return pl.pallas_call(
        _rope_kernel,
        grid=(n // tm,),
        in_specs=[
            pl.BlockSpec((tm, 1), lambda i: (i, 0)),
            pl.BlockSpec((1, LANES), lambda i: (0, 0)),
            pl.BlockSpec((1, LANES), lambda i: (0, 0)),
        ],
        out_specs=[pl.BlockSpec((tm, LANES), lambda i: (i, 0))] * 2,
        out_shape=[jax.ShapeDtypeStruct((n, LANES), F32)] * 2,
        compiler_params=pltpu.CompilerParams(dimension_semantics=("arbitrary",)),
        name="rope_tables",
    )(positions.reshape(n, 1), inv_lane, sgn_lane)


def _modulated_norm(x, gain, scale, shift):
    ms = jnp.mean(x * x, axis=-1, keepdims=True)
    return x * lax.rsqrt(ms + NORM_EPS) * gain * (1.0 + scale) + shift


def _inproj_kernel(x_ref, g_ref, sc_ref, sh_ref, w_ref, o_ref):
    h = _modulated_norm(x_ref[...], g_ref[...], sc_ref[...], sh_ref[...])
    o_ref[...] = _dot(h.astype(BF16), w_ref[...])


def _mod_spec(layer, which, tiles_per_batch):
    return pl.BlockSpec((None, None, None, 1, D_MODEL),
                        lambda i: (layer, which, i // tiles_per_batch, 0, 0))


def _inproj(x2, mod5, gain, w_pad, layer, seq):
    n = x2.shape[0]
    tm = min(512, seq)
    tpb = seq // tm
    return pl.pallas_call(
        _inproj_kernel,
        grid=(n // tm,),
        in_specs=[
            pl.BlockSpec((tm, D_MODEL), lambda i: (i, 0)),
            pl.BlockSpec((1, D_MODEL), lambda i: (0, 0)),
            _mod_spec(layer, 1, tpb),
            _mod_spec(layer, 0, tpb),
            pl.BlockSpec((D_MODEL, D_IN_PAD), lambda i: (0, 0)),
        ],
        out_specs=pl.BlockSpec((tm, D_IN_PAD), lambda i: (i, 0)),
        out_shape=jax.ShapeDtypeStruct((n, D_IN_PAD), F32),
        compiler_params=pltpu.CompilerParams(
            dimension_semantics=("arbitrary",), vmem_limit_bytes=VMEM_LIMIT),
        name="inproj",
    )(x2, gain, mod5, mod5, w_pad)


def _attn_consts():
    i = np.arange(LANES)
    within = i % HEAD_DIM
    half = ROPE_DIM // 2
    src = np.where(within < half, i + half, np.where(within < ROPE_DIM, i - half, -1))
    rot = (i[:, None] == src[None, :]).astype(np.float32)
    swap = (i[:, None] == ((i + HEAD_DIM) % LANES)[None, :]).astype(np.float32)
    return jnp.asarray(rot, BF16), jnp.asarray(swap, BF16)


def _attn_kernel(sink_ref, q_ref, k_ref, v_ref, cos_ref, sin_ref, qg_ref, kg_ref, bd_ref,
                 rot_ref, swap_ref, o_ref, kvar_ref, vvar_ref):
    n = pl.program_id(0)
    w = WINDOW
    nb = q_ref.shape[0]
    rows = nb * w

    @pl.when(n == 0)
    def _():
        kvar_ref[:, :, 0:w, :] = jnp.zeros((4, nb, w, LANES), BF16)
        vvar_ref[:, :, 0:w, :] = jnp.zeros((4, nb, w, LANES), BF16)

    @pl.when(n > 0)
    def _():
        kvar_ref[:, :, 0:w, :] = kvar_ref[:, :, w:2 * w, :]
        vvar_ref[:, :, 0:w, :] = vvar_ref[:, :, w:2 * w, :]

    cos = cos_ref[...].reshape(rows, LANES)
    sin = sin_ref[...].reshape(rows, LANES)
    bd = bd_ref[...]
    rot = rot_ref[...]
    swap = swap_ref[...]
    lo = lax.broadcasted_iota(jnp.int32, (rows, LANES), 1) < HEAD_DIM

    def norm_rope(xp, gain):
        ss = _dot(jnp.square(xp).astype(BF16), bd)
        y = xp * lax.rsqrt(ss * (1.0 / HEAD_DIM) + NORM_EPS) * gain
        return y * cos + _dot(y.astype(BF16), rot) * sin

    def variants(x):
        xb = x.astype(BF16)
        xs = _dot(xb, swap).astype(BF16)
        zero = jnp.zeros_like(xb)
        return [jnp.where(lo, xb, zero), jnp.where(lo, zero, xs),
                jnp.where(lo, xs, zero), jnp.where(lo, zero, xb)]

    kn = norm_rope(k_ref[...].reshape(rows, LANES), kg_ref[...])
    for i, t in enumerate(variants(kn)):
        kvar_ref[i, :, w:2 * w, :] = t.reshape(nb, w, LANES)
    for i, t in enumerate(variants(v_ref[...].reshape(rows, LANES))):
        vvar_ref[i, :, w:2 * w, :] = t.reshape(nb, w, LANES)

    qi = lax.broadcasted_iota(jnp.int32, (1, w, 2 * w), 1)
    kj = lax.broadcasted_iota(jnp.int32, (1, w, 2 * w), 2)
    delta = qi + w - kj
    valid = (delta >= 0) & (delta < w) & ((n * w + kj - w) >= 0)
    bias = jnp.where(valid, 0.0, -jnp.inf).astype(F32)

    for p in range(A_Q_HEADS // 2):
        g = (2 * p) // (A_Q_HEADS // A_KV_HEADS)
        qn = norm_rope(q_ref[:, :, p * LANES:(p + 1) * LANES].reshape(rows, LANES), qg_ref[...])
        qb = (qn * (HEAD_DIM ** -0.5 * LOG2E)).astype(BF16).reshape(nb, w, LANES)
        out = None
        for par in range(2):
            sink = sink_ref[2 * p + par] * LOG2E
            s = _dot(qb, kvar_ref[2 * g + par], BNT) + bias
            m = jnp.maximum(jnp.max(s, axis=-1, keepdims=True), sink)
            e = jnp.exp2(s - m)
            den = jnp.sum(e, axis=-1, keepdims=True) + jnp.exp2(sink - m)
            o = _dot(e.astype(BF16), vvar_ref[2 * g + par], BNN) / den
            out = o if out is None else out + o
        o_ref[:, :, p * LANES:(p + 1) * LANES] = out.astype(o_ref.dtype)


def _attention(proj3, cos_t, sin_t, q_norm, k_norm, sinks, bd128):
    batch, seq, _ = proj3.shape
    nblk = seq // WINDOW
    qw = A_Q_HEADS * HEAD_DIM
    qg = jnp.tile(q_norm.astype(F32), 2).reshape(1, LANES)
    kg = jnp.tile(k_norm.astype(F32), 2).reshape(1, LANES)
    const = lambda shape: pl.BlockSpec(shape, lambda i: (0, 0))
    return pl.pallas_call(
        _attn_kernel,
        grid=(nblk,),
        in_specs=[
            pl.BlockSpec(memory_space=pltpu.SMEM),
            pl.BlockSpec((batch, WINDOW, qw), lambda i: (0, i, 0)),
            pl.BlockSpec((batch, WINDOW, LANES), lambda i: (0, i, qw // LANES)),
            pl.BlockSpec((batch, WINDOW, LANES), lambda i: (0, i, qw // LANES + 1)),
            pl.BlockSpec((batch, WINDOW, LANES), lambda i: (0, i, 0)),
            pl.BlockSpec((batch, WINDOW, LANES), lambda i: (0, i, 0)),
            const((1, LANES)), const((1, LANES)), const((LANES, LANES)),
            const((LANES, LANES)), const((LANES, LANES)),
        ],
        out_specs=pl.BlockSpec((batch, WINDOW, qw), lambda i: (0, i, 0)),
        out_shape=jax.ShapeDtypeStruct((batch, seq, qw), BF16),
        scratch_shapes=[pltpu.VMEM((4, batch, 2 * WINDOW, LANES), BF16),
                        pltpu.VMEM((4, batch, 2 * WINDOW, LANES), BF16)],
        compiler_params=pltpu.CompilerParams(
            dimension_semantics=("arbitrary",), vmem_limit_bytes=VMEM_LIMIT),
        name="attention",
    )(sinks.astype(F32), proj3, proj3, proj3, cos_t, sin_t, qg, kg, bd128, *_attn_consts())


def _hgrn_consts():
    c = CHUNK
    t = np.arange(c)[:, None]
    r = np.arange(c)[None, :]
    mats = [(r <= t).astype(np.float32), (r > t).astype(np.float32)]
    for l in range(6):
        ref = ((t >> (l + 1)) << (l + 1)) + (1 << l) - 1
        diff = (r <= t).astype(np.float32) - (r <= ref).astype(np.float32)
        mats.append(np.where(t > ref, diff, -diff))
    mall = np.concatenate(mats, axis=0)
    s = np.arange(LANES)[None, :] % c
    x = t ^ s
    lvl = np.where(t > s, np.floor(np.log2(np.maximum(x, 1))).astype(np.int32),
                   np.where(t == s, -1, -2)).astype(np.int32)
    return jnp.asarray(mall, BF16), jnp.asarray(lvl, jnp.int32)


def _head_block_mask(nh):
    ri = lax.broadcasted_iota(jnp.int32, (nh, nh), 0) // HEAD_DIM
    ci = lax.broadcasted_iota(jnp.int32, (nh, nh), 1) // HEAD_DIM
    return ri == ci


def _hgrn_kernel(q_ref, z_ref, v_ref, gt_ref, lbl_ref, gain_ref, mall_ref, lvl_ref, bd_ref,
                 o_ref, st_ref, *, layer):
    c = CHUNK
    nh = B_HEADS * HEAD_DIM
    nb = q_ref.shape[0]
    rows = nb * c

    @pl.when(pl.program_id(0) == 0)
    def _():
        st_ref[...] = jnp.zeros_like(st_ref)

    q = q_ref[...].reshape(rows, nh)
    z = z_ref[...].reshape(rows, nh)
    v = v_ref[...].reshape(rows, nh)

    lg = lbl_ref[...]
    e = jnp.exp(lg - jnp.max(lg, axis=0, keepdims=True))
    sm = e / jnp.sum(e, axis=0, keepdims=True)
    lb = jnp.zeros((1, nh), F32)
    for j in range(1, layer + 1):
        lb = lb + sm[j:j + 1, :]

    a = jnp.log(lb)
    b = jnp.log1p(-lb) + jnp.minimum(z, 0.0) - jnp.log1p(jnp.exp(-jnp.abs(z)))
    lf = jnp.maximum(a, b) + jnp.log1p(jnp.exp(-jnp.abs(a - b)))
    kk = (1.0 - lb) * jax.nn.sigmoid(-z)

    pieces = _split(lf.reshape(nb, c, nh), 3)
    m_cr = jnp.broadcast_to(mall_ref[0:2 * c, :][None], (nb, 2 * c, c))
    m_lv = jnp.broadcast_to(mall_ref[2 * c:8 * c, :][None], (nb, 6 * c, c))
    cr = _dot(m_cr, pieces[0], BNN) + _dot(m_cr, pieces[1], BNN) + _dot(m_cr, pieces[2], BNN)
    dlv = _dot(m_lv, pieces[0], BNN) + _dot(m_lv, pieces[1], BNN)
    cum = cr[:, 0:c]
    rest = cr[:, c:2 * c]

    q3 = q.reshape(nb, c, nh)
    k3 = kk.reshape(nb, c, nh)
    vb = v.reshape(nb, c, nh).astype(BF16)
    lvl = lvl_ref[...][None]
    trow = lax.broadcasted_iota(jnp.int32, (1, c, LANES), 1)
    zero = jnp.zeros((nb, c, LANES), F32)
    o_intra = []
    for p in range(B_HEADS // 2):
        sl = slice(p * LANES, (p + 1) * LANES)
        qp, kp = q3[:, :, sl], k3[:, :, sl]
        pm = jnp.where(lvl == -1, _dot(qp.astype(BF16), _blockdiag2(kp.astype(BF16)), BNT), zero)
        for l in range(6):
            dec = jnp.exp(dlv[:, l * c:(l + 1) * c, sl])
            is_q = ((trow >> l) & 1) == 1
            xl = (jnp.where(is_q, qp, kp) * dec).astype(BF16)
            pm = jnp.where(lvl == l, _dot(xl, _blockdiag2(xl), BNT), pm)
        o_intra.append(_dot(pm.astype(BF16), _blockdiag2(vb[:, :, sl]), BNN))

    q_dec = (q3 * jnp.exp(cum)).astype(BF16)
    k_dec = (k3 * jnp.exp(rest)).astype(BF16)
    dec_last = jnp.exp(cum[:, c - 1:c, :])
    blk = _head_block_mask(nh)
    o_inter = []
    for i in range(nb):
        st = st_ref[i]
        o_inter.append(_dot(q_dec[i], st.astype(BF16), NT))
        upd = _dot(vb[i], k_dec[i], TN)
        st_ref[i] = st * dec_last[i] + jnp.where(blk, upd, jnp.zeros_like(upd))
    o = (jnp.concatenate(o_intra, axis=2) + jnp.stack(o_inter, axis=0)).reshape(rows, nh)

    ss = _dot(jnp.square(o).astype(BF16), bd_ref[...])
    y = o * lax.rsqrt(ss * (1.0 / HEAD_DIM) + NORM_EPS) * gain_ref[...]
    o_ref[...] = (y * _silu(gt_ref[...].reshape(rows, nh))).reshape(nb, c, nh).astype(o_ref.dtype)


def _hgrn(proj3, lb_logits, gain, bd256, layer):
    batch, seq, _ = proj3.shape
    nc = seq // CHUNK
    nh = B_HEADS * HEAD_DIM
    depth = lb_logits.shape[0]
    mall, lvl = _hgrn_consts()
    col0 = (A_Q_HEADS + 2 * A_KV_HEADS) * HEAD_DIM // nh
    const = lambda shape: pl.BlockSpec(shape, lambda i: (0, 0))
    return pl.pallas_call(
        functools.partial(_hgrn_kernel, layer=layer),
        grid=(nc,),
        in_specs=[pl.BlockSpec((batch, CHUNK, nh), lambda i, k=k: (0, i, col0 + k)) for k in range(4)]
        + [const((depth, nh)), const((1, nh)), const(mall.shape), const(lvl.shape), const((nh, nh))],
        out_specs=pl.BlockSpec((batch, CHUNK, nh), lambda i: (0, i, 0)),
        out_shape=jax.ShapeDtypeStruct((batch, seq, nh), BF16),
        scratch_shapes=[pltpu.VMEM((batch, nh, nh), F32)],
        compiler_params=pltpu.CompilerParams(
            dimension_semantics=("arbitrary",), vmem_limit_bytes=VMEM_LIMIT),
        name="hgrn2",
    )(proj3, proj3, proj3, proj3, lb_logits.astype(F32),
      jnp.tile(gain.astype(F32), B_HEADS).reshape(1, nh), mall, lvl, bd256)


def _gdn_consts():
    c = CHUNK
    t = np.arange(c)[:, None]
    r = np.arange(c)[None, :]
    tri2 = np.concatenate([(r <= t), (r > t)], axis=0).astype(np.float32)
    nh = C_HEADS * HEAD_DIM
    ex = np.zeros((LANES, 2 * nh), np.float32)
    for h in range(C_HEADS):
        ex[h, h * HEAD_DIM:(h + 1) * HEAD_DIM] = 1.0
        ex[C_HEADS + h, nh + h * HEAD_DIM:nh + (h + 1) * HEAD_DIM] = 1.0
    return jnp.asarray(tri2, BF16), jnp.asarray(ex, BF16)


def _gdn_kernel(cq_ref, ck_ref, cv_ref, cg_ref, gate_ref, cw_ref, alog_ref, dt_ref, gain_ref,
                tri_ref, ex_ref, bd_ref, o_ref, xbuf_ref, s_ref):
    c = CHUNK
    nh = C_HEADS * HEAD_DIM
    nb = cq_ref.shape[0]
    rows = nb * c
    tail = SUBLANES

    @pl.when(pl.program_id(0) == 0)
    def _():
        s_ref[...] = jnp.zeros_like(s_ref)
        xbuf_ref[:, 0:tail, :] = jnp.zeros((nb, tail, 3 * nh), F32)

    xbuf_ref[:, tail:tail + c, 0:nh] = cq_ref[...]
    xbuf_ref[:, tail:tail + c, nh:2 * nh] = ck_ref[...]
    xbuf_ref[:, tail:tail + c, 2 * nh:3 * nh] = cv_ref[...]
    w = cw_ref[...]
    y = None
    for j in range(CONV_WIDTH):
        off = tail - (CONV_WIDTH - 1) + j
        t = xbuf_ref[:, off:off + c, :] * w[j:j + 1, :][None]
        y = t if y is None else y + t
    xbuf_ref[:, 0:tail, :] = xbuf_ref[:, c:c + tail, :]
    y = _silu(y).reshape(rows, 3 * nh)
    q, k, v = y[:, 0:nh], y[:, nh:2 * nh], y[:, 2 * nh:3 * nh]

    bd = bd_ref[...]
    q = q * lax.rsqrt(_dot(jnp.square(q).astype(BF16), bd) + NORM_EPS) * (HEAD_DIM ** -0.5)
    k = k * lax.rsqrt(_dot(jnp.square(k).astype(BF16), bd) + NORM_EPS)

    gates = _dot_exact_r(gate_ref[...].reshape(rows, LANES), ex_ref[...])
    beta = jax.nn.sigmoid(gates[:, 0:nh])
    xg = gates[:, nh:2 * nh] + dt_ref[...]
    g = -jnp.exp(alog_ref[...]) * (jnp.maximum(xg, 0.0) + jnp.log1p(jnp.exp(-jnp.abs(xg))))

    g3 = g.reshape(nb, c, nh)
    tri2 = jnp.broadcast_to(tri_ref[...][None], (nb, 2 * c, c))
    gsum = None
    for piece in _split(g3, 3):
        t = _dot(tri2, piece, BNN)
        gsum = t if gsum is None else gsum + t
    gcum = gsum[:, 0:c]
    grest = gsum[:, c:2 * c]
    exp_g = jnp.exp(gcum)
    q3 = q.reshape(nb, c, nh)
    k3 = k.reshape(nb, c, nh)
    beta3 = beta.reshape(nb, c, nh)
    kb = k3 * beta3
    vb = v.reshape(nb, c, nh) * beta3
    kbg = kb * exp_g

    ti = lax.broadcasted_iota(jnp.int32, (1, c, LANES), 1)
    si = lax.broadcasted_iota(jnp.int32, (1, c, LANES), 2) & (HEAD_DIM - 1)
    incl = ti >= si
    strict = ti > si
    zero = jnp.zeros((nb, c, LANES), F32)
    eye = jnp.where(ti == si, 1.0, 0.0).astype(F32)
    same16 = (ti >> 4) == (si >> 4)

    def pmm(a, b):
        return _dot(a.astype(BF16), _blockdiag2(b.astype(BF16)), BNN)

    us, ws, qks = [], [], []
    for p in range(C_HEADS // 2):
        sl = slice(p * LANES, (p + 1) * LANES)
        ldiff = None
        for piece in _split(jnp.where(strict, g3[:, :, sl], zero), 3):
            t = _dot(tri2[:, 0:c], piece, BNN)
            ldiff = t if ldiff is None else ldiff + t
        lmat = jnp.where(incl, jnp.exp(jnp.where(incl, ldiff, zero)), zero)
        kbd = _blockdiag2(k3[:, :, sl].astype(BF16))
        amat = jnp.where(strict, _dot(kb[:, :, sl].astype(BF16), kbd, BNT) * lmat, zero)
        dmat = jnp.where(same16, amat, zero)
        noff = amat - dmat
        bm = -dmat
        b2 = pmm(bm, bm)
        b4 = pmm(b2, b2)
        b8 = pmm(b4, b4)
        td = eye + bm
        td = td + pmm(td, b2)
        td = td + pmm(td, b4)
        td = td + pmm(td, b8)
        mm = pmm(td, noff)
        m2 = pmm(mm, mm)
        imm = eye - mm
        tinv = pmm(imm + pmm(imm, m2), td).astype(BF16)
        us.append(_dot(tinv, _blockdiag2(vb[:, :, sl].astype(BF16)), BNN))
        ws.append(_dot(tinv, _blockdiag2(kbg[:, :, sl].astype(BF16)), BNN))
        qk = _dot(q3[:, :, sl].astype(BF16), kbd, BNT)
        qks.append(jnp.where(incl, qk * lmat, zero).astype(BF16))
    u = jnp.concatenate(us, axis=2)
    wmat = jnp.concatenate(ws, axis=2).astype(BF16)
    q_dec = (q3 * exp_g).astype(BF16)
    k_dec = (k3 * jnp.exp(grest)).astype(BF16)
    dec_last = jnp.exp(gcum[:, c - 1:c, :])
    blk = _head_block_mask(nh)
    outs = []
    for i in range(nb):
        s = s_ref[i]
        sb = s.astype(BF16)
        v_new = (u[i] - _dot(wmat[i], sb)).astype(BF16)
        o = _dot(q_dec[i], sb)
        o = o + jnp.concatenate(
            [_dot(qks[p][i], _blockdiag2(v_new[:, p * LANES:(p + 1) * LANES]))
             for p in range(C_HEADS // 2)], axis=1)
        outs.append(o)
        upd = _dot(k_dec[i], v_new, TN)
        s_ref[i] = s * dec_last[i] + jnp.where(blk, upd, jnp.zeros_like(upd))
    o = jnp.stack(outs, axis=0).reshape(rows, nh)

    ss = _dot(jnp.square(o).astype(BF16), bd)
    yo = o * lax.rsqrt(ss * (1.0 / HEAD_DIM) + NORM_EPS) * gain_ref[...]
    o_ref[...] = (yo * _silu(cg_ref[...].reshape(rows, nh))).reshape(nb, c, nh).astype(o_ref.dtype)


def _gdn(proj3, conv_w, a_log, dt_bias, gain, bd256):
    batch, seq, _ = proj3.shape
    nc = seq // CHUNK
    nh = C_HEADS * HEAD_DIM
    tri2, ex = _gdn_consts()
    col0 = ((A_Q_HEADS + 2 * A_KV_HEADS) * HEAD_DIM + 4 * nh) // nh
    const = lambda shape: pl.BlockSpec(shape, lambda i: (0, 0))
    rep = lambda a: jnp.repeat(a.astype(F32), HEAD_DIM).reshape(1, nh)
    return pl.pallas_call(
        _gdn_kernel,
        grid=(nc,),
        in_specs=[pl.BlockSpec((batch, CHUNK, nh), lambda i, k=k: (0, i, col0 + k)) for k in range(4)]
        + [pl.BlockSpec((batch, CHUNK, LANES), lambda i: (0, i, GATE_COL // LANES)),
           const((CONV_WIDTH, 3 * nh)), const((1, nh)), const((1, nh)), const((1, nh)),
           const(tri2.shape), const(ex.shape), const((nh, nh))],
        out_specs=pl.BlockSpec((batch, CHUNK, nh), lambda i: (0, i, 0)),
        out_shape=jax.ShapeDtypeStruct((batch, seq, nh), BF16),
        scratch_shapes=[pltpu.VMEM((batch, CHUNK + SUBLANES, 3 * nh), F32),
                        pltpu.VMEM((batch, nh, nh), F32)],
        compiler_params=pltpu.CompilerParams(
            dimension_semantics=("arbitrary",), vmem_limit_bytes=VMEM_LIMIT),
        name="gated_deltanet",
    )(proj3, proj3, proj3, proj3, proj3, conv_w.astype(F32), rep(a_log), rep(dt_bias),
      jnp.tile(gain.astype(F32), C_HEADS).reshape(1, nh), tri2, ex, bd256)


def _outproj_kernel(a_ref, b_ref, c_ref, x_ref, gt_ref, g_ref, sc_ref, sh_ref, w_ref, x1_ref, h_ref):
    wa = A_Q_HEADS * HEAD_DIM
    wb = wa + B_HEADS * HEAD_DIM
    y = (_dot(a_ref[...], w_ref[0:wa, :]) + _dot(b_ref[...], w_ref[wa:wb, :])
         + _dot(c_ref[...], w_ref[wb:D_MIX, :]))
    x1 = x_ref[...] + gt_ref[...] * y
    x1_ref[...] = x1
    h_ref[...] = _modulated_norm(x1, g_ref[...], sc_ref[...], sh_ref[...]).astype(h_ref.dtype)


def _outproj(out_a, out_b, out_c, x2, mod5, gain, w_out, layer, seq):
    n = x2.shape[0]
    tm = min(512, seq)
    tpb = seq // tm
    tile = lambda width: pl.BlockSpec((tm, width), lambda i: (i, 0))
    return pl.pallas_call(
        _outproj_kernel,
        grid=(n // tm,),
        in_specs=[tile(out_a.shape[1]), tile(out_b.shape[1]), tile(out_c.shape[1]), tile(D_MODEL),
                  _mod_spec(layer, 2, tpb), pl.BlockSpec((1, D_MODEL), lambda i: (0, 0)),
                  _mod_spec(layer, 4, tpb), _mod_spec(layer, 3, tpb),
                  pl.BlockSpec((D_MIX, D_MODEL), lambda i: (0, 0))],
        out_specs=[tile(D_MODEL), tile(D_MODEL)],
        out_shape=[jax.ShapeDtypeStruct((n, D_MODEL), F32), jax.ShapeDtypeStruct((n, D_MODEL), BF16)],
        compiler_params=pltpu.CompilerParams(
            dimension_semantics=("arbitrary",), vmem_limit_bytes=VMEM_LIMIT),
        name="outproj",
    )(out_a, out_b, out_c, x2, mod5, gain, mod5, mod5, w_out)


def _ffn_kernel(h_ref, x_ref, gt_ref, wg_ref, wu_ref, wd_ref, o_ref):
    h = h_ref[...]
    act = (_silu(_dot(h, wg_ref[...])) * _dot(h, wu_ref[...])).astype(BF16)
    o_ref[...] = x_ref[...] + gt_ref[...] * _dot(act, wd_ref[...])


def _ffn(h2, x1, mod5, w_gate, w_up, w_down, layer, seq):
    n = x1.shape[0]
    tm = min(512, seq)
    tpb = seq // tm
    resident = lambda shape: pl.BlockSpec(shape, lambda i: (0, 0), pipeline_mode=pl.Buffered(1))
    return pl.pallas_call(
        _ffn_kernel,
        grid=(n // tm,),
        in_specs=[
            pl.BlockSpec((tm, D_MODEL), lambda i: (i, 0)),
            pl.BlockSpec((tm, D_MODEL), lambda i: (i, 0)),
            _mod_spec(layer, 5, tpb),
            resident((D_MODEL, D_FF)),
            resident((D_MODEL, D_FF)),
            resident((D_FF, D_MODEL)),
        ],
        out_specs=pl.BlockSpec((tm, D_MODEL), lambda i: (i, 0)),
        out_shape=jax.ShapeDtypeStruct((n, D_MODEL), F32),
        compiler_params=pltpu.CompilerParams(
            dimension_semantics=("arbitrary",), vmem_limit_bytes=VMEM_LIMIT),
        name="ffn",
    )(h2, x1, mod5, w_gate, w_up, w_down)


def _blockdiag_ones(size):
    i = np.arange(size) // HEAD_DIM
    return jnp.asarray((i[:, None] == i[None, :]).astype(np.float32), BF16)


def kernel(x, c, positions, ada_w, ada_b, norm_mix, w_in, attn_q_norm, attn_k_norm, attn_sinks,
           hgrn_lb_logits, hgrn_out_norm, gdn_conv_w, gdn_a_log, gdn_dt_bias, gdn_out_norm, w_out,
           norm_ffn, w_gate, w_up, w_down):
    batch, seq, _ = x.shape
    depth = ada_w.shape[0]
    n = batch * seq
    x2 = x.reshape(n, D_MODEL).astype(F32)

    mod = _modulation(c.astype(F32), ada_w.astype(F32), ada_b.astype(F32))
    mod5 = mod.reshape(depth, 6, batch, 1, D_MODEL)
    cos_t, sin_t = _rope_tables(positions)
    cos3 = cos_t.reshape(batch, seq, LANES)
    sin3 = sin_t.reshape(batch, seq, LANES)
    bd128 = _blockdiag_ones(LANES)
    bd256 = _blockdiag_ones(B_HEADS * HEAD_DIM)

    for l in range(depth):
        w_pad = jnp.pad(w_in[l].astype(BF16), ((0, 0), (0, D_IN_PAD - D_IN)))
        proj = _inproj(x2, mod5, norm_mix[l].astype(F32).reshape(1, D_MODEL), w_pad, l, seq)
        proj3 = proj.reshape(batch, seq, D_IN_PAD)
        out_a = _attention(proj3, cos3, sin3, attn_q_norm[l], attn_k_norm[l], attn_sinks[l],
                           bd128).reshape(n, -1)
        out_b = _hgrn(proj3, hgrn_lb_logits, hgrn_out_norm[l], bd256, l).reshape(n, -1)
        out_c = _gdn(proj3, gdn_conv_w[l], gdn_a_log[l], gdn_dt_bias[l], gdn_out_norm[l],
                     bd256).reshape(n, -1)
        x1, h2 = _outproj(out_a, out_b, out_c, x2, mod5, norm_ffn[l].astype(F32).reshape(1, D_MODEL),
                          w_out[l].astype(BF16), l, seq)
        x2 = _ffn(h2, x1, mod5, w_gate[l].astype(BF16), w_up[l].astype(BF16), w_down[l].astype(BF16),
                  l, seq)
    return x2.reshape(batch, seq, D_MODEL).astype(x.dtype)
```

```python
import functools
import math

import numpy as np
import jax
import jax.numpy as jnp
from jax import lax
from jax.experimental import pallas as pl
from jax.experimental.pallas import tpu as pltpu

F32 = jnp.float32
BF16 = jnp.bfloat16

D_MODEL = 1024
HEAD_DIM = 64
A_Q_HEADS = 8
A_KV_HEADS = 2
WINDOW = 128
ROPE_DIM = HEAD_DIM // 4
ROPE_THETA = 500000.0
B_HEADS = 4
C_HEADS = 4
CONV_WIDTH = 4
CHUNK = 64
D_FF = 2816
D_MIX = 1024
D_IN = 2824
NORM_EPS = 1e-6
LOG2E = math.log2(math.e)

LANES = 128
SUBLANES = 8
D_IN_PAD = 2944
GATE_COL = 2816
VMEM_LIMIT = 56 * 1024 * 1024
CHUNKS_PER_STEP = 1

NN = (((1,), (0,)), ((), ()))
NT = (((1,), (1,)), ((), ()))
TN = (((0,), (0,)), ((), ()))
BNN = (((2,), (1,)), ((0,), (0,)))
BNT = (((2,), (2,)), ((0,), (0,)))


def _dot(a, b, dims=NN):
    return lax.dot_general(a, b, dims, preferred_element_type=F32)


def _split(x, n):
    parts = []
    r = x
    for i in range(n):
        p = r.astype(BF16)
        parts.append(p)
        if i + 1 < n:
            r = r - p.astype(F32)
    return parts


def _dot_exact_r(x, m, n=3):
    out = None
    for p in _split(x, n):
        t = _dot(p, m)
        out = t if out is None else out + t
    return out


def _dot3(a, b, dims=NN):
    a1, a2 = _split(a, 2)
    b1, b2 = _split(b, 2)
    return _dot(a1, b1, dims) + _dot(a1, b2, dims) + _dot(a2, b1, dims)


def _silu(x):
    return x * jax.nn.sigmoid(x)


def _lane_lo(shape):
    return (lax.broadcasted_iota(jnp.int32, shape, len(shape) - 1) & (LANES - 1)) < HEAD_DIM


def _blockdiag2(x):
    lo = _lane_lo(x.shape)
    zero = jnp.zeros_like(x)
    return jnp.concatenate([jnp.where(lo, x, zero), jnp.where(lo, zero, x)], axis=x.ndim - 2)


def _blockdiag2_t(x):
    r = x.shape[1]
    xt = jnp.swapaxes(jnp.concatenate([x, x], axis=1), 1, 2)
    row_lo = lax.broadcasted_iota(jnp.int32, xt.shape, 1) < HEAD_DIM
    col_lo = lax.broadcasted_iota(jnp.int32, xt.shape, 2) < r
    return jnp.where(row_lo == col_lo, xt, jnp.zeros_like(xt))


def _head_rmsnorm(o, bd, gain):
    ss = _dot(jnp.square(o).astype(BF16), bd)
    return o * lax.rsqrt(ss * (1.0 / HEAD_DIM) + NORM_EPS) * gain


def _mod_kernel(c_ref, w_ref, b_ref, o_ref):
    c = c_ref[...]
    o_ref[...] = _dot3(_silu(c), w_ref[...]) + b_ref[...]


def _modulation(c, ada_w, ada_b):
    depth = ada_w.shape[0]
    b = c.shape[0]
    return pl.pallas_call(
        _mod_kernel,
        grid=(depth, 6),
        in_specs=[
            pl.BlockSpec((b, D_MODEL), lambda l, k: (0, 0)),
            pl.BlockSpec((None, D_MODEL, D_MODEL), lambda l, k: (l, 0, k)),
            pl.BlockSpec((None, None, 1, D_MODEL), lambda l, k: (l, k, 0, 0)),
        ],
        out_specs=pl.BlockSpec((None, None, b, D_MODEL), lambda l, k: (l, k, 0, 0)),
        out_shape=jax.ShapeDtypeStruct((depth, 6, b, D_MODEL), F32),
        compiler_params=pltpu.CompilerParams(
            dimension_semantics=("arbitrary", "arbitrary"), vmem_limit_bytes=VMEM_LIMIT),
        name="modulation",
    )(c, ada_w, ada_b.reshape(depth, 6, 1, D_MODEL))


def _rope_kernel(pos_ref, inv_ref, sgn_ref, cos_ref, sin_ref):
    ang = pos_ref[...].astype(F32) * inv_ref[...]
    cos_ref[...] = jnp.cos(ang)
    sin_ref[...] = jnp.sin(ang) * sgn_ref[...]


def _rope_tables(positions):
    n = positions.size
    tm = min(2048, n)
    half = ROPE_DIM // 2
    inv_freq = ROPE_THETA ** (-jnp.arange(half, dtype=F32) * 2.0 / ROPE_DIM)
    lane = np.arange(LANES) % HEAD_DIM
    inv_lane = jnp.where(lane < ROPE_DIM, inv_freq[lane % half], 0.0).reshape(1, LANES).astype(F32)
    sgn_lane = jnp.asarray(np.where(lane < half, -1.0, np.where(lane < ROPE_DIM, 1.0, 0.0)),
                           F32).reshape(1, LANES)
    return pl.pallas_call(
        _rope_kernel,
        grid=(n // tm,),
        in_specs=[
            pl.BlockSpec((tm, 1), lambda i: (i, 0)),
            pl.BlockSpec((1, LANES), lambda i: (0, 0)),
            pl.BlockSpec((1, LANES), lambda i: (0, 0)),
        ],
        out_specs=[pl.BlockSpec((tm, LANES), lambda i: (i, 0))] * 2,
        out_shape=[jax.ShapeDtypeStruct((n, LANES), F32)] * 2,
        compiler_params=pltpu.CompilerParams(dimension_semantics=("arbitrary",)),
        name="rope_tables",
    )(positions.reshape(n, 1), inv_lane, sgn_lane)


def _modulated_norm(x, gain, scale, shift):
    ms = jnp.mean(x * x, axis=-1, keepdims=True)
    return x * lax.rsqrt(ms + NORM_EPS) * gain * (1.0 + scale) + shift


def _inproj_kernel(x_ref, g_ref, sc_ref, sh_ref, w_ref, o_ref):
    h = _modulated_norm(x_ref[...], g_ref[...], sc_ref[...], sh_ref[...])
    o_ref[...] = _dot(h.astype(BF16), w_ref[...])


def _mod_spec(layer, which, tiles_per_batch):
    return pl.BlockSpec((None, None, None, 1, D_MODEL),
                        lambda i: (layer, which, i // tiles_per_batch, 0, 0))


def _inproj(x2, mod5, gain, w_pad, layer, seq):
    n = x2.shape[0]
    tm = min(512, seq)
    tpb = seq // tm
    return pl.pallas_call(
        _inproj_kernel,
        grid=(n // tm,),
        in_specs=[
            pl.BlockSpec((tm, D_MODEL), lambda i: (i, 0)),
            pl.BlockSpec((1, D_MODEL), lambda i: (0, 0)),
            _mod_spec(layer, 1, tpb),
            _mod_spec(layer, 0, tpb),
            pl.BlockSpec((D_MODEL, D_IN_PAD), lambda i: (0, 0)),
        ],
        out_specs=pl.BlockSpec((tm, D_IN_PAD), lambda i: (i, 0)),
        out_shape=jax.ShapeDtypeStruct((n, D_IN_PAD), F32),
        compiler_params=pltpu.CompilerParams(
            dimension_semantics=("arbitrary",), vmem_limit_bytes=VMEM_LIMIT),
        name="inproj",
    )(x2, gain, mod5, mod5, w_pad)


def _attn_consts():
    i = np.arange(LANES)
    within = i % HEAD_DIM
    half = ROPE_DIM // 2
    src = np.where(within < half, i + half, np.where(within < ROPE_DIM, i - half, -1))
    rot = (i[:, None] == src[None, :]).astype(np.float32)
    swap = (i[:, None] == ((i + HEAD_DIM) % LANES)[None, :]).astype(np.float32)
    return jnp.asarray(rot, BF16), jnp.asarray(swap, BF16)


def _attn_kernel(sink_ref, q_ref, k_ref, v_ref, cos_ref, sin_ref, qg_ref, kg_ref, bd_ref,
                 rot_ref, swap_ref, o_ref, kvar_ref, vvar_ref):
    n = pl.program_id(0)
    w = WINDOW
    nb = q_ref.shape[0]
    rows = nb * w

    @pl.when(n == 0)
    def _():
        kvar_ref[:, :, 0:w, :] = jnp.zeros((4, nb, w, LANES), BF16)
        vvar_ref[:, :, 0:w, :] = jnp.zeros((4, nb, w, LANES), BF16)

    @pl.when(n > 0)
    def _():
        kvar_ref[:, :, 0:w, :] = kvar_ref[:, :, w:2 * w, :]
        vvar_ref[:, :, 0:w, :] = vvar_ref[:, :, w:2 * w, :]

    cos = cos_ref[...].reshape(rows, LANES)
    sin = sin_ref[...].reshape(rows, LANES)
    bd = bd_ref[...]
    rot = rot_ref[...]
    swap = swap_ref[...]
    lo = lax.broadcasted_iota(jnp.int32, (rows, LANES), 1) < HEAD_DIM

    def norm_rope(xp, gain):
        y = _head_rmsnorm(xp, bd, gain)
        return y * cos + _dot(y.astype(BF16), rot) * sin

    def variants(x):
        xb = x.astype(BF16)
        xs = _dot(xb, swap).astype(BF16)
        zero = jnp.zeros_like(xb)
        return [jnp.where(lo, xb, zero), jnp.where(lo, zero, xs),
                jnp.where(lo, xs, zero), jnp.where(lo, zero, xb)]

    kn = norm_rope(k_ref[...].reshape(rows, LANES), kg_ref[...])
    for i, t in enumerate(variants(kn)):
        kvar_ref[i, :, w:2 * w, :] = t.reshape(nb, w, LANES)
    for i, t in enumerate(variants(v_ref[...].reshape(rows, LANES))):
        vvar_ref[i, :, w:2 * w, :] = t.reshape(nb, w, LANES)

    qi = lax.broadcasted_iota(jnp.int32, (1, w, 2 * w), 1)
    kj = lax.broadcasted_iota(jnp.int32, (1, w, 2 * w), 2)
    delta = qi + w - kj
    valid = (delta >= 0) & (delta < w) & ((n * w + kj - w) >= 0)
    bias = jnp.where(valid, 0.0, -jnp.inf).astype(F32)

    for p in range(A_Q_HEADS // 2):
        g = (2 * p) // (A_Q_HEADS // A_KV_HEADS)
        qn = norm_rope(q_ref[:, :, p * LANES:(p + 1) * LANES].reshape(rows, LANES), qg_ref[...])
        qb = (qn * (HEAD_DIM ** -0.5 * LOG2E)).astype(BF16).reshape(nb, w, LANES)
        out = None
        for par in range(2):
            sink = sink_ref[2 * p + par] * LOG2E
            s = _dot(qb, kvar_ref[2 * g + par], BNT) + bias
            m = jnp.maximum(jnp.max(s, axis=-1, keepdims=True), sink)
            e = jnp.exp2(s - m)
            den = jnp.sum(e, axis=-1, keepdims=True) + jnp.exp2(sink - m)
            o = _dot(e.astype(BF16), vvar_ref[2 * g + par], BNN) / den
            out = o if out is None else out + o
        o_ref[:, :, p * LANES:(p + 1) * LANES] = out.astype(o_ref.dtype)


def _attention(proj3, cos_t, sin_t, q_norm, k_norm, sinks, bd128):
    batch, seq, _ = proj3.shape
    nblk = seq // WINDOW
    qw = A_Q_HEADS * HEAD_DIM
    qg = jnp.tile(q_norm.astype(F32), 2).reshape(1, LANES)
    kg = jnp.tile(k_norm.astype(F32), 2).reshape(1, LANES)
    const = lambda shape: pl.BlockSpec(shape, lambda i: (0, 0))
    return pl.pallas_call(
        _attn_kernel,
        grid=(nblk,),
        in_specs=[
            pl.BlockSpec(memory_space=pltpu.SMEM),
            pl.BlockSpec((batch, WINDOW, qw), lambda i: (0, i, 0)),
            pl.BlockSpec((batch, WINDOW, LANES), lambda i: (0, i, qw // LANES)),
            pl.BlockSpec((batch, WINDOW, LANES), lambda i: (0, i, qw // LANES + 1)),
            pl.BlockSpec((batch, WINDOW, LANES), lambda i: (0, i, 0)),
            pl.BlockSpec((batch, WINDOW, LANES), lambda i: (0, i, 0)),
            const((1, LANES)), const((1, LANES)), const((LANES, LANES)),
            const((LANES, LANES)), const((LANES, LANES)),
        ],
        out_specs=pl.BlockSpec((batch, WINDOW, qw), lambda i: (0, i, 0)),
        out_shape=jax.ShapeDtypeStruct((batch, seq, qw), BF16),
        scratch_shapes=[pltpu.VMEM((4, batch, 2 * WINDOW, LANES), BF16),
                        pltpu.VMEM((4, batch, 2 * WINDOW, LANES), BF16)],
        compiler_params=pltpu.CompilerParams(
            dimension_semantics=("arbitrary",), vmem_limit_bytes=VMEM_LIMIT),
        name="attention",
    )(sinks.astype(F32), proj3, proj3, proj3, cos_t, sin_t, qg, kg, bd128, *_attn_consts())


def _hgrn_consts():
    c = CHUNK
    t = np.arange(c)[:, None]
    r = np.arange(c)[None, :]
    mats = [(r <= t).astype(np.float32), (r > t).astype(np.float32)]
    for l in range(6):
        ref = ((t >> (l + 1)) << (l + 1)) + (1 << l) - 1
        diff = (r <= t).astype(np.float32) - (r <= ref).astype(np.float32)
        mats.append(np.where(t > ref, diff, -diff))
    mall = np.concatenate(mats, axis=0)
    s = np.arange(LANES)[None, :] % c
    x = t ^ s
    lvl = np.where(t > s, np.floor(np.log2(np.maximum(x, 1))).astype(np.int32),
                   np.where(t == s, -1, -2)).astype(np.int32)
    return jnp.asarray(mall, BF16), jnp.asarray(lvl, jnp.int32)


def _head_block_mask(nh):
    ri = lax.broadcasted_iota(jnp.int32, (nh, nh), 0) // HEAD_DIM
    ci = lax.broadcasted_iota(jnp.int32, (nh, nh), 1) // HEAD_DIM
    return ri == ci


def _hgrn_kernel(q_ref, z_ref, v_ref, gt_ref, lbl_ref, gain_ref, mall_ref, lvl_ref, bd_ref,
                 o_ref, st_ref, *, layer):
    c = CHUNK
    nh = B_HEADS * HEAD_DIM
    batch, span, _ = q_ref.shape
    cps = span // c
    nb = batch * cps
    rows = nb * c

    @pl.when(pl.program_id(0) == 0)
    def _():
        st_ref[...] = jnp.zeros_like(st_ref)

    q = q_ref[...].reshape(rows, nh)
    z = z_ref[...].reshape(rows, nh)
    v = v_ref[...].reshape(rows, nh)

    lg = lbl_ref[...]
    e = jnp.exp(lg - jnp.max(lg, axis=0, keepdims=True))
    sm = e / jnp.sum(e, axis=0, keepdims=True)
    lb = jnp.zeros((1, nh), F32)
    for j in range(1, layer + 1):
        lb = lb + sm[j:j + 1, :]

    a = jnp.log(lb)
    b = jnp.log1p(-lb) + jnp.minimum(z, 0.0) - jnp.log1p(jnp.exp(-jnp.abs(z)))
    lf = jnp.maximum(a, b) + jnp.log1p(jnp.exp(-jnp.abs(a - b)))
    kk = (1.0 - lb) * jax.nn.sigmoid(-z)

    pieces = _split(lf.reshape(nb, c, nh), 3)
    m_cr = jnp.broadcast_to(mall_ref[0:2 * c, :][None], (nb, 2 * c, c))
    m_lv = jnp.broadcast_to(mall_ref[2 * c:8 * c, :][None], (nb, 6 * c, c))
    cr = _dot(m_cr, pieces[0], BNN) + _dot(m_cr, pieces[1], BNN) + _dot(m_cr, pieces[2], BNN)
    dlv = _dot(m_lv, pieces[0], BNN) + _dot(m_lv, pieces[1], BNN)
    cum = cr[:, 0:c]
    rest = cr[:, c:2 * c]

    q3 = q.reshape(nb, c, nh)
    k3 = kk.reshape(nb, c, nh)
    vb = v.reshape(nb, c, nh).astype(BF16)
    lvl = lvl_ref[...][None]
    trow = lax.broadcasted_iota(jnp.int32, (1, c, LANES), 1)

    def intra(lo_i, hi_i, p, result):
        sl = slice(p * LANES, (p + 1) * LANES)
        qp, kp = q3[lo_i:hi_i, :, sl], k3[lo_i:hi_i, :, sl]
        zero = jnp.zeros(qp.shape, F32)
        lhs, rhs = qp.astype(BF16), _blockdiag2_t(kp.astype(BF16))
        yield
        raw = _dot(lhs, rhs, BNN)
        yield
        pm = jnp.where(lvl == -1, raw, zero)
        for l in range(6):
            dec = jnp.exp(dlv[lo_i:hi_i, l * c:(l + 1) * c, sl])
            is_q = ((trow >> l) & 1) == 1
            xl = (jnp.where(is_q, qp, kp) * dec).astype(BF16)
            rhs = _blockdiag2_t(xl)
            yield
            raw = _dot(xl, rhs, BNN)
            yield
            pm = jnp.where(lvl == l, raw, pm)
        lhs, rhs = pm.astype(BF16), _blockdiag2(vb[lo_i:hi_i, :, sl])
        yield
        result[(lo_i, p)] = _dot(lhs, rhs, BNN)

    half = max(nb // 2, 1)
    result = {}
    streams = [intra(lo_i, min(lo_i + half, nb), p, result)
               for lo_i in range(0, nb, half) for p in range(B_HEADS // 2)]
    live = []
    pending = list(streams)
    while pending or live:
        if pending:
            live.append(pending.pop(0))
        for s in list(live):
            try:
                next(s)
            except StopIteration:
                live.remove(s)
    o_intra = [jnp.concatenate([result[(lo_i, p)] for lo_i in range(0, nb, half)], axis=0)
               for p in range(B_HEADS // 2)]

    q_dec = (q3 * jnp.exp(cum)).astype(BF16)
    k_dec = (k3 * jnp.exp(rest)).astype(BF16)
    dec_last = jnp.exp(cum[:, c - 1:c, :])
    blk = _head_block_mask(nh)
    o_inter = []
    for bi in range(batch):
        st = st_ref[bi]
        for ci in range(cps):
            i = bi * cps + ci
            o_inter.append(_dot(q_dec[i], st.astype(BF16), NT))
            upd = _dot(vb[i], k_dec[i], TN)
            st = st * dec_last[i] + jnp.where(blk, upd, jnp.zeros_like(upd))
        st_ref[bi] = st
    o = (jnp.concatenate(o_intra, axis=2) + jnp.stack(o_inter, axis=0)).reshape(rows, nh)

    y = _head_rmsnorm(o, bd_ref[...], gain_ref[...])
    o_ref[...] = (y * _silu(gt_ref[...].reshape(rows, nh))).reshape(batch, span, nh).astype(o_ref.dtype)


def _hgrn(proj3, lb_logits, gain, bd256, layer):
    batch, seq, _ = proj3.shape
    span = CHUNK * min(CHUNKS_PER_STEP, seq // CHUNK)
    nh = B_HEADS * HEAD_DIM
    depth = lb_logits.shape[0]
    mall, lvl = _hgrn_consts()
    col0 = (A_Q_HEADS + 2 * A_KV_HEADS) * HEAD_DIM // nh
    const = lambda shape: pl.BlockSpec(shape, lambda i: (0, 0))
    return pl.pallas_call(
        functools.partial(_hgrn_kernel, layer=layer),
        grid=(seq // span,),
        in_specs=[pl.BlockSpec((batch, span, nh), lambda i, k=k: (0, i, col0 + k)) for k in range(4)]
        + [const((depth, nh)), const((1, nh)), const(mall.shape), const(lvl.shape), const((nh, nh))],
        out_specs=pl.BlockSpec((batch, span, nh), lambda i: (0, i, 0)),
        out_shape=jax.ShapeDtypeStruct((batch, seq, nh), BF16),
        scratch_shapes=[pltpu.VMEM((batch, nh, nh), F32)],
        compiler_params=pltpu.CompilerParams(
            dimension_semantics=("arbitrary",), vmem_limit_bytes=VMEM_LIMIT),
        name="hgrn2",
    )(proj3, proj3, proj3, proj3, lb_logits.astype(F32),
      jnp.tile(gain.astype(F32), B_HEADS).reshape(1, nh), mall, lvl, bd256)


def _gdn_consts():
    c = CHUNK
    t = np.arange(c)[:, None]
    r = np.arange(c)[None, :]
    tri2 = np.concatenate([(r <= t), (r > t)], axis=0).astype(np.float32)
    nh = C_HEADS * HEAD_DIM
    ex = np.zeros((LANES, 2 * nh), np.float32)
    for h in range(C_HEADS):
        ex[h, h * HEAD_DIM:(h + 1) * HEAD_DIM] = 1.0
        ex[C_HEADS + h, nh + h * HEAD_DIM:nh + (h + 1) * HEAD_DIM] = 1.0
    return jnp.asarray(tri2, BF16), jnp.asarray(ex, BF16)


def _gdn_kernel(cq_ref, ck_ref, cv_ref, cg_ref, gate_ref, cw_ref, alog_ref, dt_ref, gain_ref,
                tri_ref, ex_ref, bd_ref, o_ref, xbuf_ref, s_ref):
    c = CHUNK
    nh = C_HEADS * HEAD_DIM
    batch, span, _ = cq_ref.shape
    cps = span // c
    nb = batch * cps
    rows = nb * c
    tail = SUBLANES

    @pl.when(pl.program_id(0) == 0)
    def _():
        s_ref[...] = jnp.zeros_like(s_ref)
        xbuf_ref[:, 0:tail, :] = jnp.zeros((batch, tail, 3 * nh), F32)

    xbuf_ref[:, tail:tail + span, 0:nh] = cq_ref[...]
    xbuf_ref[:, tail:tail + span, nh:2 * nh] = ck_ref[...]
    xbuf_ref[:, tail:tail + span, 2 * nh:3 * nh] = cv_ref[...]
    w = cw_ref[...]
    y = None
    for j in range(CONV_WIDTH):
        off = tail - (CONV_WIDTH - 1) + j
        t = xbuf_ref[:, off:off + span, :] * w[j:j + 1, :][None]
        y = t if y is None else y + t
    xbuf_ref[:, 0:tail, :] = xbuf_ref[:, span:span + tail, :]
    y = _silu(y).reshape(rows, 3 * nh)
    q, k, v = y[:, 0:nh], y[:, nh:2 * nh], y[:, 2 * nh:3 * nh]

    bd = bd_ref[...]
    q = q * lax.rsqrt(_dot(jnp.square(q).astype(BF16), bd) + NORM_EPS) * (HEAD_DIM ** -0.5)
    k = k * lax.rsqrt(_dot(jnp.square(k).astype(BF16), bd) + NORM_EPS)

    gates = _dot_exact_r(gate_ref[...].reshape(rows, LANES), ex_ref[...])
    beta = jax.nn.sigmoid(gates[:, 0:nh])
    xg = gates[:, nh:2 * nh] + dt_ref[...]
    g = -jnp.exp(alog_ref[...]) * (jnp.maximum(xg, 0.0) + jnp.log1p(jnp.exp(-jnp.abs(xg))))

    g3 = g.reshape(nb, c, nh)
    tri2 = jnp.broadcast_to(tri_ref[...][None], (nb, 2 * c, c))
    gsum = None
    for piece in _split(g3, 3):
        t = _dot(tri2, piece, BNN)
        gsum = t if gsum is None else gsum + t
    gcum = gsum[:, 0:c]
    grest = gsum[:, c:2 * c]
    exp_g = jnp.exp(gcum)
    q3 = q.reshape(nb, c, nh)
    k3 = k.reshape(nb, c, nh)
    beta3 = beta.reshape(nb, c, nh)
    kb = k3 * beta3
    vb = v.reshape(nb, c, nh) * beta3
    kbg = kb * exp_g

    ti = lax.broadcasted_iota(jnp.int32, (1, c, LANES), 1)
    si = lax.broadcasted_iota(jnp.int32, (1, c, LANES), 2) & (HEAD_DIM - 1)
    incl = ti >= si
    strict = ti > si
    zero = jnp.zeros((nb, c, LANES), F32)
    eye = jnp.where(ti == si, 1.0, 0.0).astype(F32)
    same16 = (ti >> 4) == (si >> 4)

    def pmm(a, b):
        return _dot(a.astype(BF16), _blockdiag2(b.astype(BF16)), BNN)

    def pmm2(a, y):
        yb = y.astype(BF16)
        rhs = jnp.concatenate([_blockdiag2(yb[:, :, 0:LANES]), _blockdiag2(yb[:, :, LANES:2 * LANES])], axis=2)
        return _dot(a.astype(BF16), rhs, BNN)

    us, ws, qks = [], [], []
    for p in range(C_HEADS // 2):
        sl = slice(p * LANES, (p + 1) * LANES)
        ldiff = None
        for piece in _split(jnp.where(strict, g3[:, :, sl], zero), 3):
            t = _dot(tri2[:, 0:c], piece, BNN)
            ldiff = t if ldiff is None else ldiff + t
        lmat = jnp.where(incl, jnp.exp(jnp.where(incl, ldiff, zero)), zero)
        kbd_t = _blockdiag2_t(k3[:, :, sl].astype(BF16))
        amat = jnp.where(strict, _dot(kb[:, :, sl].astype(BF16), kbd_t, BNN) * lmat, zero)
        dmat = jnp.where(same16, amat, zero)
        noff = amat - dmat
        bm = -dmat
        b2 = pmm(bm, bm)
        b4 = pmm(b2, b2)
        b8 = pmm(b4, b4)
        td = eye + bm
        td = td + pmm(td, b2)
        td = td + pmm(td, b4)
        td = td + pmm(td, b8)
        mm = pmm(td, noff)
        y0 = pmm2(td, jnp.concatenate([vb[:, :, sl], kbg[:, :, sl]], axis=2))
        m2 = pmm(mm, mm)
        y1 = y0 + pmm2(m2, y0)
        y2 = y1 - pmm2(mm, y1)
        us.append(y2[:, :, 0:LANES])
        ws.append(y2[:, :, LANES:2 * LANES])
        qk = _dot(q3[:, :, sl].astype(BF16), kbd_t, BNN)
        qks.append(jnp.where(incl, qk * lmat, zero).astype(BF16))
    u = jnp.concatenate(us, axis=2)
    wmat = jnp.concatenate(ws, axis=2).astype(BF16)
    q_dec = (q3 * exp_g).astype(BF16)
    k_dec = (k3 * jnp.exp(grest)).astype(BF16)
    dec_last = jnp.exp(gcum[:, c - 1:c, :])
    blk = _head_block_mask(nh)
    outs = []
    for bi in range(batch):
        s = s_ref[bi]
        for ci in range(cps):
            i = bi * cps + ci
            sb = s.astype(BF16)
            v_new = (u[i] - _dot(wmat[i], sb)).astype(BF16)
            o = _dot(q_dec[i], sb)
            o = o + jnp.concatenate(
                [_dot(qks[p][i], _blockdiag2(v_new[:, p * LANES:(p + 1) * LANES]))
                 for p in range(C_HEADS // 2)], axis=1)
            outs.append(o)
            upd = _dot(k_dec[i], v_new, TN)
            s = s * dec_last[i] + jnp.where(blk, upd, jnp.zeros_like(upd))
        s_ref[bi] = s
    o = jnp.stack(outs, axis=0).reshape(rows, nh)

    yo = _head_rmsnorm(o, bd, gain_ref[...])
    o_ref[...] = (yo * _silu(cg_ref[...].reshape(rows, nh))).reshape(batch, span, nh).astype(o_ref.dtype)


def _gdn(proj3, conv_w, a_log, dt_bias, gain, bd256):
    batch, seq, _ = proj3.shape
    span = CHUNK * min(CHUNKS_PER_STEP, seq // CHUNK)
    nh = C_HEADS * HEAD_DIM
    tri2, ex = _gdn_consts()
    col0 = ((A_Q_HEADS + 2 * A_KV_HEADS) * HEAD_DIM + 4 * nh) // nh
    const = lambda shape: pl.BlockSpec(shape, lambda i: (0, 0))
    rep = lambda a: jnp.repeat(a.astype(F32), HEAD_DIM).reshape(1, nh)
    return pl.pallas_call(
        _gdn_kernel,
        grid=(seq // span,),
        in_specs=[pl.BlockSpec((batch, span, nh), lambda i, k=k: (0, i, col0 + k)) for k in range(4)]
        + [pl.BlockSpec((batch, span, LANES), lambda i: (0, i, GATE_COL // LANES)),
           const((CONV_WIDTH, 3 * nh)), const((1, nh)), const((1, nh)), const((1, nh)),
           const(tri2.shape), const(ex.shape), const((nh, nh))],
        out_specs=pl.BlockSpec((batch, span, nh), lambda i: (0, i, 0)),
        out_shape=jax.ShapeDtypeStruct((batch, seq, nh), BF16),
        scratch_shapes=[pltpu.VMEM((batch, span + SUBLANES, 3 * nh), F32),
                        pltpu.VMEM((batch, nh, nh), F32)],
        compiler_params=pltpu.CompilerParams(
            dimension_semantics=("arbitrary",), vmem_limit_bytes=VMEM_LIMIT),
        name="gated_deltanet",
    )(proj3, proj3, proj3, proj3, proj3, conv_w.astype(F32), rep(a_log), rep(dt_bias),
      jnp.tile(gain.astype(F32), C_HEADS).reshape(1, nh), tri2, ex, bd256)


def _outproj_ffn_kernel(a_ref, b_ref, c_ref, x_ref, gtm_ref, g_ref, sc_ref, sh_ref, gtf_ref,
                        wo_ref, wg_ref, wu_ref, wd_ref, o_ref):
    wa = A_Q_HEADS * HEAD_DIM
    wb = wa + B_HEADS * HEAD_DIM
    y = (_dot(a_ref[...], wo_ref[0:wa, :]) + _dot(b_ref[...], wo_ref[wa:wb, :])
         + _dot(c_ref[...], wo_ref[wb:D_MIX, :]))
    x1 = x_ref[...] + gtm_ref[...] * y
    h = _modulated_norm(x1, g_ref[...], sc_ref[...], sh_ref[...]).astype(BF16)
    act = (_silu(_dot(h, wg_ref[...])) * _dot(h, wu_ref[...])).astype(BF16)
    o_ref[...] = x1 + gtf_ref[...] * _dot(act, wd_ref[...])


def _outproj_ffn(out_a, out_b, out_c, x2, mod5, gain, w_out, w_gate, w_up, w_down, layer, seq):
    n = x2.shape[0]
    tm = min(512, seq)
    tpb = seq // tm
    tile = lambda width: pl.BlockSpec((tm, width), lambda i: (i, 0))
    resident = lambda shape: pl.BlockSpec(shape, lambda i: (0, 0), pipeline_mode=pl.Buffered(1))
    return pl.pallas_call(
        _outproj_ffn_kernel,
        grid=(n // tm,),
        in_specs=[tile(out_a.shape[1]), tile(out_b.shape[1]), tile(out_c.shape[1]), tile(D_MODEL),
                  _mod_spec(layer, 2, tpb), resident((1, D_MODEL)),
                  _mod_spec(layer, 4, tpb), _mod_spec(layer, 3, tpb), _mod_spec(layer, 5, tpb),
                  resident((D_MIX, D_MODEL)), resident((D_MODEL, D_FF)), resident((D_MODEL, D_FF)),
                  resident((D_FF, D_MODEL))],
        out_specs=tile(D_MODEL),
        out_shape=jax.ShapeDtypeStruct((n, D_MODEL), F32),
        compiler_params=pltpu.CompilerParams(
            dimension_semantics=("arbitrary",), vmem_limit_bytes=VMEM_LIMIT),
        name="outproj_ffn",
    )(out_a, out_b, out_c, x2, mod5, gain, mod5, mod5, mod5, w_out, w_gate, w_up, w_down)


def _blockdiag_ones(size):
    i = np.arange(size) // HEAD_DIM
    return jnp.asarray((i[:, None] == i[None, :]).astype(np.float32), BF16)


def kernel(x, c, positions, ada_w, ada_b, norm_mix, w_in, attn_q_norm, attn_k_norm, attn_sinks,
           hgrn_lb_logits, hgrn_out_norm, gdn_conv_w, gdn_a_log, gdn_dt_bias, gdn_out_norm, w_out,
           norm_ffn, w_gate, w_up, w_down):
    batch, seq, _ = x.shape
    depth = ada_w.shape[0]
    n = batch * seq
    x2 = x.reshape(n, D_MODEL).astype(F32)

    mod = _modulation(c.astype(F32), ada_w.astype(F32), ada_b.astype(F32))
    mod5 = mod.reshape(depth, 6, batch, 1, D_MODEL)
    cos_t, sin_t = _rope_tables(positions)
    cos3 = cos_t.reshape(batch, seq, LANES)
    sin3 = sin_t.reshape(batch, seq, LANES)
    bd128 = _blockdiag_ones(LANES)
    bd256 = _blockdiag_ones(B_HEADS * HEAD_DIM)

    for l in range(depth):
        w_pad = jnp.pad(w_in[l].astype(BF16), ((0, 0), (0, D_IN_PAD - D_IN)))
        proj = _inproj(x2, mod5, norm_mix[l].astype(F32).reshape(1, D_MODEL), w_pad, l, seq)
        proj3 = proj.reshape(batch, seq, D_IN_PAD)
        out_a = _attention(proj3, cos3, sin3, attn_q_norm[l], attn_k_norm[l], attn_sinks[l],
                           bd128).reshape(n, -1)
        out_b = _hgrn(proj3, hgrn_lb_logits, hgrn_out_norm[l], bd256, l).reshape(n, -1)
        out_c = _gdn(proj3, gdn_conv_w[l], gdn_a_log[l], gdn_dt_bias[l], gdn_out_norm[l],
                     bd256).reshape(n, -1)
        x2 = _outproj_ffn(out_a, out_b, out_c, x2, mod5, norm_ffn[l].astype(F32).reshape(1, D_MODEL),
                          w_out[l].astype(BF16), w_gate[l].astype(BF16), w_up[l].astype(BF16),
                          w_down[l].astype(BF16), l, seq)
    return x2.reshape(batch, seq, D_MODEL).astype(x.dtype)
```

```python
import functools
import math

import numpy as np
import jax
import jax.numpy as jnp
from jax import lax
from jax.experimental import pallas as pl
from jax.experimental.pallas import tpu as pltpu

F32 = jnp.float32
BF16 = jnp.bfloat16

D_MODEL = 1024
HEAD_DIM = 64
A_Q_HEADS = 8
A_KV_HEADS = 2
WINDOW = 128
ROPE_DIM = HEAD_DIM // 4
ROPE_THETA = 500000.0
B_HEADS = 4
C_HEADS = 4
CONV_WIDTH = 4
CHUNK = 64
D_FF = 2816
D_MIX = 1024
D_IN = 2824
NORM_EPS = 1e-6
LOG2E = math.log2(math.e)

LANES = 128
SUBLANES = 8
D_IN_PAD = 2944
GATE_COL = 2816
VMEM_LIMIT = 56 * 1024 * 1024

NN = (((1,), (0,)), ((), ()))
NT = (((1,), (1,)), ((), ()))
TN = (((0,), (0,)), ((), ()))
BNN = (((2,), (1,)), ((0,), (0,)))
BNT = (((2,), (2,)), ((0,), (0,)))


def _dot(a, b, dims=NN):
    return lax.dot_general(a, b, dims, preferred_element_type=F32)


def _split(x, n):
    parts = []
    r = x
    for i in range(n):
        p = r.astype(BF16)
        parts.append(p)
        if i + 1 < n:
            r = r - p.astype(F32)
    return parts


def _dot_exact_r(x, m, n=3):
    out = None
    for p in _split(x, n):
        t = _dot(p, m)
        out = t if out is None else out + t
    return out


def _dot3(a, b, dims=NN):
    a1, a2 = _split(a, 2)
    b1, b2 = _split(b, 2)
    return _dot(a1, b1, dims) + _dot(a1, b2, dims) + _dot(a2, b1, dims)


def _silu(x):
    return x * jax.nn.sigmoid(x)


def _lane_lo(shape):
    return (lax.broadcasted_iota(jnp.int32, shape, len(shape) - 1) & (LANES - 1)) < HEAD_DIM


def _blockdiag2(x):
    lo = _lane_lo(x.shape)
    zero = jnp.zeros_like(x)
    return jnp.concatenate([jnp.where(lo, x, zero), jnp.where(lo, zero, x)], axis=x.ndim - 2)


def _blockdiag2_t(x):
    r = x.shape[1]
    xt = jnp.swapaxes(jnp.concatenate([x, x], axis=1), 1, 2)
    row_lo = lax.broadcasted_iota(jnp.int32, xt.shape, 1) < HEAD_DIM
    col_lo = lax.broadcasted_iota(jnp.int32, xt.shape, 2) < r
    return jnp.where(row_lo == col_lo, xt, jnp.zeros_like(xt))


def _head_rmsnorm(o, bd, gain):
    ss = _dot(jnp.square(o).astype(BF16), bd)
    return o * lax.rsqrt(ss * (1.0 / HEAD_DIM) + NORM_EPS) * gain


def _mod_kernel(c_ref, w_ref, b_ref, o_ref):
    c = c_ref[...]
    o_ref[...] = _dot3(_silu(c), w_ref[...]) + b_ref[...]


def _modulation(c, ada_w, ada_b):
    depth = ada_w.shape[0]
    b = c.shape[0]
    return pl.pallas_call(
        _mod_kernel,
        grid=(depth, 6),
        in_specs=[
            pl.BlockSpec((b, D_MODEL), lambda l, k: (0, 0)),
            pl.BlockSpec((None, D_MODEL, D_MODEL), lambda l, k: (l, 0, k)),
            pl.BlockSpec((None, None, 1, D_MODEL), lambda l, k: (l, k, 0, 0)),
        ],
        out_specs=pl.BlockSpec((None, None, b, D_MODEL), lambda l, k: (l, k, 0, 0)),
        out_shape=jax.ShapeDtypeStruct((depth, 6, b, D_MODEL), F32),
        compiler_params=pltpu.CompilerParams(
            dimension_semantics=("arbitrary", "arbitrary"), vmem_limit_bytes=VMEM_LIMIT),
        name="modulation",
    )(c, ada_w, ada_b.reshape(depth, 6, 1, D_MODEL))


def _rope_kernel(pos_ref, inv_ref, sgn_ref, cos_ref, sin_ref):
    ang = pos_ref[...].astype(F32) * inv_ref[...]
    cos_ref[...] = jnp.cos(ang)
    sin_ref[...] = jnp.sin(ang) * sgn_ref[...]


def _rope_tables(positions):
    n = positions.size
    tm = min(2048, n)
    half = ROPE_DIM // 2
    inv_freq = ROPE_THETA ** (-jnp.arange(half, dtype=F32) * 2.0 / ROPE_DIM)
    lane = np.arange(LANES) % HEAD_DIM
    inv_lane = jnp.where(lane < ROPE_DIM, inv_freq[lane % half], 0.0).reshape(1, LANES).astype(F32)
    sgn_lane = jnp.asarray(np.where(lane < half, -1.0, np.where(lane < ROPE_DIM, 1.0, 0.0)),
                           F32).reshape(1, LANES)
    return pl.pallas_call(
        _rope_kernel,
        grid=(n // tm,),
        in_specs=[
            pl.BlockSpec((tm, 1), lambda i: (i, 0)),
            pl.BlockSpec((1, LANES), lambda i: (0, 0)),
            pl.BlockSpec((1, LANES), lambda i: (0, 0)),
        ],
        out_specs=[pl.BlockSpec((tm, LANES), lambda i: (i, 0))] * 2,
        out_shape=[jax.ShapeDtypeStruct((n, LANES), F32)] * 2,
        compiler_params=pltpu.CompilerParams(dimension_semantics=("arbitrary",)),
        name="rope_tables",
    )(positions.reshape(n, 1), inv_lane, sgn_lane)


def _modulated_norm(x, gain, scale, shift):
    ms = jnp.mean(x * x, axis=-1, keepdims=True)
    return x * lax.rsqrt(ms + NORM_EPS) * gain * (1.0 + scale) + shift


def _inproj_kernel(x_ref, g_ref, sc_ref, sh_ref, w_ref, o_ref):
    h = _modulated_norm(x_ref[...], g_ref[...], sc_ref[...], sh_ref[...])
    o_ref[...] = _dot(h.astype(BF16), w_ref[...])


def _mod_spec(layer, which, tiles_per_batch):
    return pl.BlockSpec((None, None, None, 1, D_MODEL),
                        lambda i: (layer, which, i // tiles_per_batch, 0, 0))


def _inproj(x2, mod5, gain, w_pad, layer, seq):
    n = x2.shape[0]
    tm = min(512, seq)
    tpb = seq // tm
    return pl.pallas_call(
        _inproj_kernel,
        grid=(n // tm,),
        in_specs=[
            pl.BlockSpec((tm, D_MODEL), lambda i: (i, 0)),
            pl.BlockSpec((1, D_MODEL), lambda i: (0, 0)),
            _mod_spec(layer, 1, tpb),
            _mod_spec(layer, 0, tpb),
            pl.BlockSpec((D_MODEL, D_IN_PAD), lambda i: (0, 0)),
        ],
        out_specs=pl.BlockSpec((tm, D_IN_PAD), lambda i: (i, 0)),
        out_shape=jax.ShapeDtypeStruct((n, D_IN_PAD), F32),
        compiler_params=pltpu.CompilerParams(
            dimension_semantics=("arbitrary",), vmem_limit_bytes=VMEM_LIMIT),
        name="inproj",
    )(x2, gain, mod5, mod5, w_pad)


def _attn_consts():
    i = np.arange(LANES)
    within = i % HEAD_DIM
    half = ROPE_DIM // 2
    src = np.where(within < half, i + half, np.where(within < ROPE_DIM, i - half, -1))
    rot = (i[:, None] == src[None, :]).astype(np.float32)
    swap = (i[:, None] == ((i + HEAD_DIM) % LANES)[None, :]).astype(np.float32)
    return jnp.asarray(rot, BF16), jnp.asarray(swap, BF16)


def _attn_kernel(sink_ref, q_ref, k_ref, v_ref, cos_ref, sin_ref, qg_ref, kg_ref, bd_ref,
                 rot_ref, swap_ref, o_ref, kvar_ref, vvar_ref):
    n = pl.program_id(0)
    w = WINDOW
    nb = q_ref.shape[0]
    rows = nb * w

    @pl.when(n == 0)
    def _():
        kvar_ref[:, :, 0:w, :] = jnp.zeros((4, nb, w, LANES), BF16)
        vvar_ref[:, :, 0:w, :] = jnp.zeros((4, nb, w, LANES), BF16)

    @pl.when(n > 0)
    def _():
        kvar_ref[:, :, 0:w, :] = kvar_ref[:, :, w:2 * w, :]
        vvar_ref[:, :, 0:w, :] = vvar_ref[:, :, w:2 * w, :]

    cos = cos_ref[...].reshape(rows, LANES)
    sin = sin_ref[...].reshape(rows, LANES)
    bd = bd_ref[...]
    rot = rot_ref[...]
    swap = swap_ref[...]
    lo = lax.broadcasted_iota(jnp.int32, (rows, LANES), 1) < HEAD_DIM

    def norm_rope(xp, gain):
        y = _head_rmsnorm(xp, bd, gain)
        return y * cos + _dot(y.astype(BF16), rot) * sin

    def variants(x):
        xb = x.astype(BF16)
        xs = _dot(xb, swap).astype(BF16)
        zero = jnp.zeros_like(xb)
        return [jnp.where(lo, xb, zero), jnp.where(lo, zero, xs),
                jnp.where(lo, xs, zero), jnp.where(lo, zero, xb)]

    kn = norm_rope(k_ref[...].reshape(rows, LANES), kg_ref[...])
    for i, t in enumerate(variants(kn)):
        kvar_ref[i, :, w:2 * w, :] = t.reshape(nb, w, LANES)
    for i, t in enumerate(variants(v_ref[...].reshape(rows, LANES))):
        vvar_ref[i, :, w:2 * w, :] = t.reshape(nb, w, LANES)

    qi = lax.broadcasted_iota(jnp.int32, (1, w, 2 * w), 1)
    kj = lax.broadcasted_iota(jnp.int32, (1, w, 2 * w), 2)
    delta = qi + w - kj
    valid = (delta >= 0) & (delta < w) & ((n * w + kj - w) >= 0)
    bias = jnp.where(valid, 0.0, -jnp.inf).astype(F32)

    for p in range(A_Q_HEADS // 2):
        g = (2 * p) // (A_Q_HEADS // A_KV_HEADS)
        qn = norm_rope(q_ref[:, :, p * LANES:(p + 1) * LANES].reshape(rows, LANES), qg_ref[...])
        qb = (qn * (HEAD_DIM ** -0.5 * LOG2E)).astype(BF16).reshape(nb, w, LANES)
        out = None
        for par in range(2):
            sink = sink_ref[2 * p + par] * LOG2E
            s = _dot(qb, kvar_ref[2 * g + par], BNT) + bias
            m = jnp.maximum(jnp.max(s, axis=-1, keepdims=True), sink)
            e = jnp.exp2(s - m)
            den = jnp.sum(e, axis=-1, keepdims=True) + jnp.exp2(sink - m)
            o = _dot(e.astype(BF16), vvar_ref[2 * g + par], BNN) / den
            out = o if out is None else out + o
        o_ref[:, :, p * LANES:(p + 1) * LANES] = out.astype(o_ref.dtype)


def _attention(proj3, cos_t, sin_t, q_norm, k_norm, sinks, bd128):
    batch, seq, _ = proj3.shape
    nblk = seq // WINDOW
    qw = A_Q_HEADS * HEAD_DIM
    qg = jnp.tile(q_norm.astype(F32), 2).reshape(1, LANES)
    kg = jnp.tile(k_norm.astype(F32), 2).reshape(1, LANES)
    const = lambda shape: pl.BlockSpec(shape, lambda i: (0, 0))
    return pl.pallas_call(
        _attn_kernel,
        grid=(nblk,),
        in_specs=[
            pl.BlockSpec(memory_space=pltpu.SMEM),
            pl.BlockSpec((batch, WINDOW, qw), lambda i: (0, i, 0)),
            pl.BlockSpec((batch, WINDOW, LANES), lambda i: (0, i, qw // LANES)),
            pl.BlockSpec((batch, WINDOW, LANES), lambda i: (0, i, qw // LANES + 1)),
            pl.BlockSpec((batch, WINDOW, LANES), lambda i: (0, i, 0)),
            pl.BlockSpec((batch, WINDOW, LANES), lambda i: (0, i, 0)),
            const((1, LANES)), const((1, LANES)), const((LANES, LANES)),
            const((LANES, LANES)), const((LANES, LANES)),
        ],
        out_specs=pl.BlockSpec((batch, WINDOW, qw), lambda i: (0, i, 0)),
        out_shape=jax.ShapeDtypeStruct((batch, seq, qw), BF16),
        scratch_shapes=[pltpu.VMEM((4, batch, 2 * WINDOW, LANES), BF16),
                        pltpu.VMEM((4, batch, 2 * WINDOW, LANES), BF16)],
        compiler_params=pltpu.CompilerParams(
            dimension_semantics=("arbitrary",), vmem_limit_bytes=VMEM_LIMIT),
        name="attention",
    )(sinks.astype(F32), proj3, proj3, proj3, cos_t, sin_t, qg, kg, bd128, *_attn_consts())


def _hgrn_consts():
    c = CHUNK
    t = np.arange(c)[:, None]
    r = np.arange(c)[None, :]
    mall = np.concatenate([(r <= t), (r > t)], axis=0).astype(np.float32)
    s = np.arange(LANES)[None, :] % c
    x = t ^ s
    lvl = np.where(t > s, np.floor(np.log2(np.maximum(x, 1))).astype(np.int32),
                   np.where(t == s, -1, -2)).astype(np.int32)
    return jnp.asarray(mall, BF16), jnp.asarray(lvl, jnp.int32)


def _head_block_mask(nh):
    ri = lax.broadcasted_iota(jnp.int32, (nh, nh), 0) // HEAD_DIM
    ci = lax.broadcasted_iota(jnp.int32, (nh, nh), 1) // HEAD_DIM
    return ri == ci


def _hgrn_kernel(q_ref, z_ref, v_ref, gt_ref, lbl_ref, gain_ref, mall_ref, lvl_ref, bd_ref,
                 o_ref, st_ref, *, layer):
    c = CHUNK
    nh = B_HEADS * HEAD_DIM
    batch, span, _ = q_ref.shape
    nb = batch
    rows = nb * c

    @pl.when(pl.program_id(0) == 0)
    def _():
        st_ref[...] = jnp.zeros_like(st_ref)

    q = q_ref[...].reshape(rows, nh)
    z = z_ref[...].reshape(rows, nh)
    v = v_ref[...].reshape(rows, nh)

    lg = lbl_ref[...]
    e = jnp.exp(lg - jnp.max(lg, axis=0, keepdims=True))
    sm = e / jnp.sum(e, axis=0, keepdims=True)
    lb = jnp.zeros((1, nh), F32)
    for j in range(1, layer + 1):
        lb = lb + sm[j:j + 1, :]

    a = jnp.log(lb)
    b = jnp.log1p(-lb) + jnp.minimum(z, 0.0) - jnp.log1p(jnp.exp(-jnp.abs(z)))
    lf = jnp.maximum(a, b) + jnp.log1p(jnp.exp(-jnp.abs(a - b)))
    kk = (1.0 - lb) * jax.nn.sigmoid(-z)

    lf3 = lf.reshape(nb, c, nh)
    pieces = _split(lf3, 3)
    m_cr = jnp.broadcast_to(mall_ref[0:2 * c, :][None], (nb, 2 * c, c))
    cr = _dot(m_cr, pieces[0], BNN) + _dot(m_cr, pieces[1], BNN) + _dot(m_cr, pieces[2], BNN)
    cum = cr[:, 0:c]
    rest = cr[:, c:2 * c]

    trow_full = lax.broadcasted_iota(jnp.int32, (1, c, nh), 1)
    cum8 = cum.reshape(nb, c // SUBLANES, SUBLANES, nh)
    sub = lax.broadcasted_iota(jnp.int32, (1, 1, SUBLANES, nh), 2)
    dlv = [jnp.where((trow_full & 1) == 1, lf3, jnp.zeros_like(lf3))]
    for l in range(1, 6):
        m = 1 << l
        if 2 * m > SUBLANES:
            ref = jnp.concatenate(
                [jnp.broadcast_to(cum[:, j + m - 1:j + m, :], (nb, 2 * m, nh)) for j in range(0, c, 2 * m)],
                axis=1)
        elif 2 * m == SUBLANES:
            ref = jnp.broadcast_to(cum8[:, :, m - 1:m, :], cum8.shape).reshape(nb, c, nh)
        else:
            ref = jnp.where(sub < 2 * m, jnp.broadcast_to(cum8[:, :, m - 1:m, :], cum8.shape),
                            jnp.broadcast_to(cum8[:, :, 3 * m - 1:3 * m, :], cum8.shape)).reshape(nb, c, nh)
        diff = cum - ref
        dlv.append(jnp.where(((trow_full >> l) & 1) == 1, diff, -diff))

    q3 = q.reshape(nb, c, nh)
    k3 = kk.reshape(nb, c, nh)
    vb = v.reshape(nb, c, nh).astype(BF16)
    lvl = lvl_ref[...][None]
    trow = lax.broadcasted_iota(jnp.int32, (1, c, LANES), 1)

    def intra(lo_i, hi_i, p, result):
        sl = slice(p * LANES, (p + 1) * LANES)
        qp, kp = q3[lo_i:hi_i, :, sl], k3[lo_i:hi_i, :, sl]
        zero = jnp.zeros(qp.shape, F32)
        lhs, rhs = qp.astype(BF16), _blockdiag2_t(kp.astype(BF16))
        yield
        raw = _dot(lhs, rhs, BNN)
        yield
        pm = jnp.where(lvl == -1, raw, zero)
        for l in range(6):
            dec = jnp.exp(dlv[l][lo_i:hi_i, :, sl])
            is_q = ((trow >> l) & 1) == 1
            xl = (jnp.where(is_q, qp, kp) * dec).astype(BF16)
            rhs = _blockdiag2_t(xl)
            yield
            raw = _dot(xl, rhs, BNN)
            yield
            pm = jnp.where(lvl == l, raw, pm)
        lhs, rhs = pm.astype(BF16), _blockdiag2(vb[lo_i:hi_i, :, sl])
        yield
        result[(lo_i, p)] = _dot(lhs, rhs, BNN)

    half = max(nb // 2, 1)
    result = {}
    streams = [intra(lo_i, min(lo_i + half, nb), p, result)
               for lo_i in range(0, nb, half) for p in range(B_HEADS // 2)]
    live = []
    pending = list(streams)
    while pending or live:
        if pending:
            live.append(pending.pop(0))
        for s in list(live):
            try:
                next(s)
            except StopIteration:
                live.remove(s)
    o_intra = [jnp.concatenate([result[(lo_i, p)] for lo_i in range(0, nb, half)], axis=0)
               for p in range(B_HEADS // 2)]

    q_dec = (q3 * jnp.exp(cum)).astype(BF16)
    k_dec = (k3 * jnp.exp(rest)).astype(BF16)
    dec_last = jnp.exp(cum[:, c - 1:c, :])
    blk = _head_block_mask(nh)[None]
    st = st_ref[...]
    st_t = jnp.swapaxes(st.astype(BF16), 1, 2)
    o_inter = _dot(q_dec, jnp.where(blk, st_t, jnp.zeros_like(st_t)), BNN)
    upd = _dot(jnp.swapaxes(vb, 1, 2), k_dec, BNN)
    st_ref[...] = st * dec_last + jnp.where(blk, upd, jnp.zeros_like(upd))
    o = (jnp.concatenate(o_intra, axis=2) + o_inter).reshape(rows, nh)

    y = _head_rmsnorm(o, bd_ref[...], gain_ref[...])
    o_ref[...] = (y * _silu(gt_ref[...].reshape(rows, nh))).reshape(batch, span, nh).astype(o_ref.dtype)


def _hgrn(proj3, lb_logits, gain, bd256, layer):
    batch, seq, _ = proj3.shape
    span = CHUNK
    nh = B_HEADS * HEAD_DIM
    depth = lb_logits.shape[0]
    mall, lvl = _hgrn_consts()
    col0 = (A_Q_HEADS + 2 * A_KV_HEADS) * HEAD_DIM // nh
    const = lambda shape: pl.BlockSpec(shape, lambda i: (0, 0))
    return pl.pallas_call(
        functools.partial(_hgrn_kernel, layer=layer),
        grid=(seq // span,),
        in_specs=[pl.BlockSpec((batch, span, nh), lambda i, k=k: (0, i, col0 + k)) for k in range(4)]
        + [const((depth, nh)), const((1, nh)), const(mall.shape), const(lvl.shape), const((nh, nh))],
        out_specs=pl.BlockSpec((batch, span, nh), lambda i: (0, i, 0)),
        out_shape=jax.ShapeDtypeStruct((batch, seq, nh), BF16),
        scratch_shapes=[pltpu.VMEM((batch, nh, nh), F32)],
        compiler_params=pltpu.CompilerParams(
            dimension_semantics=("arbitrary",), vmem_limit_bytes=VMEM_LIMIT),
        name="hgrn2",
    )(proj3, proj3, proj3, proj3, lb_logits.astype(F32),
      jnp.tile(gain.astype(F32), B_HEADS).reshape(1, nh), mall, lvl, bd256)


def _gdn_consts():
    c = CHUNK
    t = np.arange(c)[:, None]
    r = np.arange(c)[None, :]
    tri2 = np.concatenate([(r <= t), (r > t)], axis=0).astype(np.float32)
    return jnp.asarray(tri2, BF16)


def _gdn_kernel(cq_ref, ck_ref, cv_ref, cg_ref, gate_ref, cw_ref, alog_ref, dt_ref, gain_ref,
                tri_ref, bd_ref, o_ref, xbuf_ref, s_ref):
    c = CHUNK
    nh = C_HEADS * HEAD_DIM
    batch, span, _ = cq_ref.shape
    nb = batch
    rows = nb * c
    tail = SUBLANES

    @pl.when(pl.program_id(0) == 0)
    def _():
        s_ref[...] = jnp.zeros_like(s_ref)
        xbuf_ref[:, 0:tail, :] = jnp.zeros((batch, tail, 3 * nh), F32)

    xbuf_ref[:, tail:tail + span, 0:nh] = cq_ref[...]
    xbuf_ref[:, tail:tail + span, nh:2 * nh] = ck_ref[...]
    xbuf_ref[:, tail:tail + span, 2 * nh:3 * nh] = cv_ref[...]
    w = cw_ref[...]
    y = None
    for j in range(CONV_WIDTH):
        off = tail - (CONV_WIDTH - 1) + j
        t = xbuf_ref[:, off:off + span, :] * w[j:j + 1, :][None]
        y = t if y is None else y + t
    xbuf_ref[:, 0:tail, :] = xbuf_ref[:, span:span + tail, :]
    y = _silu(y).reshape(rows, 3 * nh)
    q, k, v = y[:, 0:nh], y[:, nh:2 * nh], y[:, 2 * nh:3 * nh]

    bd = bd_ref[...]
    q = q * lax.rsqrt(_dot(jnp.square(q).astype(BF16), bd) + NORM_EPS) * (HEAD_DIM ** -0.5)
    k = k * lax.rsqrt(_dot(jnp.square(k).astype(BF16), bd) + NORM_EPS)

    gb = gate_ref[...].reshape(rows, LANES)
    head = lax.broadcasted_iota(jnp.int32, (rows, nh), 1) // HEAD_DIM

    def per_head(col0):
        out = jnp.broadcast_to(gb[:, col0:col0 + 1], (rows, nh))
        for h in range(1, C_HEADS):
            out = jnp.where(head == h, jnp.broadcast_to(gb[:, col0 + h:col0 + h + 1], (rows, nh)), out)
        return out

    beta = jax.nn.sigmoid(per_head(0))
    xg = per_head(C_HEADS) + dt_ref[...]
    g = -jnp.exp(alog_ref[...]) * (jnp.maximum(xg, 0.0) + jnp.log1p(jnp.exp(-jnp.abs(xg))))

    g3 = g.reshape(nb, c, nh)
    tri2 = jnp.broadcast_to(tri_ref[...][None], (nb, 2 * c, c))
    g_pieces = _split(g3, 3)
    gsum = None
    for piece in g_pieces:
        t = _dot(tri2, piece, BNN)
        gsum = t if gsum is None else gsum + t
    gcum = gsum[:, 0:c]
    grest = gsum[:, c:2 * c]
    exp_g = jnp.exp(gcum)
    q3 = q.reshape(nb, c, nh)
    k3 = k.reshape(nb, c, nh)
    beta3 = beta.reshape(nb, c, nh)
    kb = k3 * beta3
    vb = v.reshape(nb, c, nh) * beta3
    kbg = kb * exp_g

    ti = lax.broadcasted_iota(jnp.int32, (1, c, LANES), 1)
    si = lax.broadcasted_iota(jnp.int32, (1, c, LANES), 2) & (HEAD_DIM - 1)
    incl = ti >= si
    strict = ti > si
    zero = jnp.zeros((nb, c, LANES), F32)
    eye = jnp.where(ti == si, 1.0, 0.0).astype(F32)
    same16 = (ti >> 4) == (si >> 4)

    def pmm(a, b):
        return _dot(a.astype(BF16), _blockdiag2(b.astype(BF16)), BNN)

    def pmm2(a, y):
        yb = y.astype(BF16)
        rhs = jnp.concatenate([_blockdiag2(yb[:, :, 0:LANES]), _blockdiag2(yb[:, :, LANES:2 * LANES])], axis=2)
        return _dot(a.astype(BF16), rhs, BNN)

    us, ws, qks = [], [], []
    for p in range(C_HEADS // 2):
        sl = slice(p * LANES, (p + 1) * LANES)
        ldiff = None
        for piece in g_pieces[0:2]:
            masked = jnp.where(strict, piece[:, :, sl], jnp.zeros((nb, c, LANES), BF16))
            t = _dot(tri2[:, 0:c], masked, BNN)
            ldiff = t if ldiff is None else ldiff + t
        lmat = jnp.where(incl, jnp.exp(jnp.where(incl, ldiff, zero)), zero)
        kbd_t = _blockdiag2_t(k3[:, :, sl].astype(BF16))
        amat = jnp.where(strict, _dot(kb[:, :, sl].astype(BF16), kbd_t, BNN) * lmat, zero)
        dmat = jnp.where(same16, amat, zero)
        noff = amat - dmat
        bm = -dmat
        b2 = pmm(bm, bm)
        b4 = pmm(b2, b2)
        b8 = pmm(b4, b4)
        td = eye + bm
        td = td + pmm(td, b2)
        td = td + pmm(td, b4)
        td = td + pmm(td, b8)
        mm = pmm(td, noff)
        y0 = pmm2(td, jnp.concatenate([vb[:, :, sl], kbg[:, :, sl]], axis=2))
        m2 = pmm(mm, mm)
        y1 = y0 + pmm2(m2, y0)
        y2 = y1 - pmm2(mm, y1)
        us.append(y2[:, :, 0:LANES])
        ws.append(y2[:, :, LANES:2 * LANES])
        qk = _dot(q3[:, :, sl].astype(BF16), kbd_t, BNN)
        qks.append(jnp.where(incl, qk * lmat, zero).astype(BF16))
    u = jnp.concatenate(us, axis=2)
    wmat = jnp.concatenate(ws, axis=2).astype(BF16)
    q_dec = (q3 * exp_g).astype(BF16)
    k_dec = (k3 * jnp.exp(grest)).astype(BF16)
    dec_last = jnp.exp(gcum[:, c - 1:c, :])
    blk = _head_block_mask(nh)[None]
    s = s_ref[...]
    sb = s.astype(BF16)
    v_new = (u - _dot(wmat, sb, BNN)).astype(BF16)
    o = _dot(q_dec, sb, BNN) + jnp.concatenate(
        [_dot(qks[p], _blockdiag2(v_new[:, :, p * LANES:(p + 1) * LANES]), BNN)
         for p in range(C_HEADS // 2)], axis=2)
    upd = _dot(jnp.swapaxes(k_dec, 1, 2), v_new, BNN)
    s_ref[...] = s * dec_last + jnp.where(blk, upd, jnp.zeros_like(upd))
    o = o.reshape(rows, nh)

    yo = _head_rmsnorm(o, bd, gain_ref[...])
    o_ref[...] = (yo * _silu(cg_ref[...].reshape(rows, nh))).reshape(batch, span, nh).astype(o_ref.dtype)


def _gdn(proj3, conv_w, a_log, dt_bias, gain, bd256):
    batch, seq, _ = proj3.shape
    span = CHUNK
    nh = C_HEADS * HEAD_DIM
    tri2 = _gdn_consts()
    col0 = ((A_Q_HEADS + 2 * A_KV_HEADS) * HEAD_DIM + 4 * nh) // nh
    const = lambda shape: pl.BlockSpec(shape, lambda i: (0, 0))
    rep = lambda a: jnp.repeat(a.astype(F32), HEAD_DIM).reshape(1, nh)
    return pl.pallas_call(
        _gdn_kernel,
        grid=(seq // span,),
        in_specs=[pl.BlockSpec((batch, span, nh), lambda i, k=k: (0, i, col0 + k)) for k in range(4)]
        + [pl.BlockSpec((batch, span, LANES), lambda i: (0, i, GATE_COL // LANES)),
           const((CONV_WIDTH, 3 * nh)), const((1, nh)), const((1, nh)), const((1, nh)),
           const(tri2.shape), const((nh, nh))],
        out_specs=pl.BlockSpec((batch, span, nh), lambda i: (0, i, 0)),
        out_shape=jax.ShapeDtypeStruct((batch, seq, nh), BF16),
        scratch_shapes=[pltpu.VMEM((batch, span + SUBLANES, 3 * nh), F32),
                        pltpu.VMEM((batch, nh, nh), F32)],
        compiler_params=pltpu.CompilerParams(
            dimension_semantics=("arbitrary",), vmem_limit_bytes=VMEM_LIMIT),
        name="gated_deltanet",
    )(proj3, proj3, proj3, proj3, proj3, conv_w.astype(F32), rep(a_log), rep(dt_bias),
      jnp.tile(gain.astype(F32), C_HEADS).reshape(1, nh), tri2, bd256)


def _outproj_ffn_kernel(a_ref, b_ref, c_ref, x_ref, gtm_ref, g_ref, sc_ref, sh_ref, gtf_ref,
                        wo_ref, wg_ref, wu_ref, wd_ref, o_ref):
    wa = A_Q_HEADS * HEAD_DIM
    wb = wa + B_HEADS * HEAD_DIM
    y = (_dot(a_ref[...], wo_ref[0:wa, :]) + _dot(b_ref[...], wo_ref[wa:wb, :])
         + _dot(c_ref[...], wo_ref[wb:D_MIX, :]))
    x1 = x_ref[...] + gtm_ref[...] * y
    h = _modulated_norm(x1, g_ref[...], sc_ref[...], sh_ref[...]).astype(BF16)
    act = (_silu(_dot(h, wg_ref[...])) * _dot(h, wu_ref[...])).astype(BF16)
    o_ref[...] = x1 + gtf_ref[...] * _dot(act, wd_ref[...])


def _outproj_ffn(out_a, out_b, out_c, x2, mod5, gain, w_out, w_gate, w_up, w_down, layer, seq):
    n = x2.shape[0]
    tm = min(512, seq)
    tpb = seq // tm
    tile = lambda width: pl.BlockSpec((tm, width), lambda i: (i, 0))
    resident = lambda shape: pl.BlockSpec(shape, lambda i: (0, 0), pipeline_mode=pl.Buffered(1))
    return pl.pallas_call(
        _outproj_ffn_kernel,
        grid=(n // tm,),
        in_specs=[tile(out_a.shape[1]), tile(out_b.shape[1]), tile(out_c.shape[1]), tile(D_MODEL),
                  _mod_spec(layer, 2, tpb), resident((1, D_MODEL)),
                  _mod_spec(layer, 4, tpb), _mod_spec(layer, 3, tpb), _mod_spec(layer, 5, tpb),
                  resident((D_MIX, D_MODEL)), resident((D_MODEL, D_FF)), resident((D_MODEL, D_FF)),
                  resident((D_FF, D_MODEL))],
        out_specs=tile(D_MODEL),
        out_shape=jax.ShapeDtypeStruct((n, D_MODEL), F32),
        compiler_params=pltpu.CompilerParams(
            dimension_semantics=("arbitrary",), vmem_limit_bytes=VMEM_LIMIT),
        name="outproj_ffn",
    )(out_a, out_b, out_c, x2, mod5, gain, mod5, mod5, mod5, w_out, w_gate, w_up, w_down)


def _blockdiag_ones(size):
    i = np.arange(size) // HEAD_DIM
    return jnp.asarray((i[:, None] == i[None, :]).astype(np.float32), BF16)


def kernel(x, c, positions, ada_w, ada_b, norm_mix, w_in, attn_q_norm, attn_k_norm, attn_sinks,
           hgrn_lb_logits, hgrn_out_norm, gdn_conv_w, gdn_a_log, gdn_dt_bias, gdn_out_norm, w_out,
           norm_ffn, w_gate, w_up, w_down):
    batch, seq, _ = x.shape
    depth = ada_w.shape[0]
    n = batch * seq
    x2 = x.reshape(n, D_MODEL).astype(F32)

    mod = _modulation(c.astype(F32), ada_w.astype(F32), ada_b.astype(F32))
    mod5 = mod.reshape(depth, 6, batch, 1, D_MODEL)
    cos_t, sin_t = _rope_tables(positions)
    cos3 = cos_t.reshape(batch, seq, LANES)
    sin3 = sin_t.reshape(batch, seq, LANES)
    bd128 = _blockdiag_ones(LANES)
    bd256 = _blockdiag_ones(B_HEADS * HEAD_DIM)

    for l in range(depth):
        w_pad = jnp.pad(w_in[l].astype(BF16), ((0, 0), (0, D_IN_PAD - D_IN)))
        proj = _inproj(x2, mod5, norm_mix[l].astype(F32).reshape(1, D_MODEL), w_pad, l, seq)
        proj3 = proj.reshape(batch, seq, D_IN_PAD)
        out_a = _attention(proj3, cos3, sin3, attn_q_norm[l], attn_k_norm[l], attn_sinks[l],
                           bd128).reshape(n, -1)
        out_b = _hgrn(proj3, hgrn_lb_logits, hgrn_out_norm[l], bd256, l).reshape(n, -1)
        out_c = _gdn(proj3, gdn_conv_w[l], gdn_a_log[l], gdn_dt_bias[l], gdn_out_norm[l],
                     bd256).reshape(n, -1)
        x2 = _outproj_ffn(out_a, out_b, out_c, x2, mod5, norm_ffn[l].astype(F32).reshape(1, D_MODEL),
                          w_out[l].astype(BF16), w_gate[l].astype(BF16), w_up[l].astype(BF16),
                          w_down[l].astype(BF16), l, seq)
    return x2.reshape(batch, seq, D_MODEL).astype(x.dtype)
```

```python
import functools
import math

import numpy as np
import jax
import jax.numpy as jnp
from jax import lax
from jax.experimental import pallas as pl
from jax.experimental.pallas import tpu as pltpu

F32 = jnp.float32
BF16 = jnp.bfloat16

D_MODEL = 1024
HEAD_DIM = 64
A_Q_HEADS = 8
A_KV_HEADS = 2
WINDOW = 128
ROPE_DIM = HEAD_DIM // 4
ROPE_THETA = 500000.0
B_HEADS = 4
C_HEADS = 4
CONV_WIDTH = 4
CHUNK = 64
D_FF = 2816
D_MIX = 1024
D_IN = 2824
NORM_EPS = 1e-6
LOG2E = math.log2(math.e)

LANES = 128
SUBLANES = 8
D_IN_PAD = 2944
GATE_COL = 2816
VMEM_LIMIT = 56 * 1024 * 1024

NN = (((1,), (0,)), ((), ()))
NT = (((1,), (1,)), ((), ()))
TN = (((0,), (0,)), ((), ()))
BNN = (((2,), (1,)), ((0,), (0,)))
BNT = (((2,), (2,)), ((0,), (0,)))


def _dot(a, b, dims=NN):
    return lax.dot_general(a, b, dims, preferred_element_type=F32)


def _split(x, n):
    parts = []
    r = x
    for i in range(n):
        p = r.astype(BF16)
        parts.append(p)
        if i + 1 < n:
            r = r - p.astype(F32)
    return parts


def _dot_exact_r(x, m, n=3):
    out = None
    for p in _split(x, n):
        t = _dot(p, m)
        out = t if out is None else out + t
    return out


def _dot3(a, b, dims=NN):
    a1, a2 = _split(a, 2)
    b1, b2 = _split(b, 2)
    return _dot(a1, b1, dims) + _dot(a1, b2, dims) + _dot(a2, b1, dims)


def _silu(x):
    return x * jax.nn.sigmoid(x)


def _lane_lo(shape):
    return (lax.broadcasted_iota(jnp.int32, shape, len(shape) - 1) & (LANES - 1)) < HEAD_DIM


def _blockdiag2(x):
    lo = _lane_lo(x.shape)
    zero = jnp.zeros_like(x)
    return jnp.concatenate([jnp.where(lo, x, zero), jnp.where(lo, zero, x)], axis=x.ndim - 2)


def _blockdiag(x):
    heads = x.shape[-1] // HEAD_DIM
    lane_head = lax.broadcasted_iota(jnp.int32, x.shape, x.ndim - 1) // HEAD_DIM
    zero = jnp.zeros_like(x)
    return jnp.concatenate([jnp.where(lane_head == i, x, zero) for i in range(heads)], axis=x.ndim - 2)


def _blockdiag_t(x):
    heads = x.shape[-1] // HEAD_DIM
    r = x.shape[1]
    xt = jnp.swapaxes(jnp.concatenate([x] * heads, axis=1), 1, 2)
    row_head = lax.broadcasted_iota(jnp.int32, xt.shape, 1) // HEAD_DIM
    col_head = lax.broadcasted_iota(jnp.int32, xt.shape, 2) // r
    return jnp.where(row_head == col_head, xt, jnp.zeros_like(xt))


def _head_rmsnorm(o, bd, gain):
    ss = _dot(jnp.square(o).astype(BF16), bd)
    return o * lax.rsqrt(ss * (1.0 / HEAD_DIM) + NORM_EPS) * gain


def _mod_kernel(c_ref, w_ref, b_ref, o_ref):
    c = c_ref[...]
    o_ref[...] = _dot3(_silu(c), w_ref[...]) + b_ref[...]


def _modulation(c, ada_w, ada_b):
    depth = ada_w.shape[0]
    b = c.shape[0]
    return pl.pallas_call(
        _mod_kernel,
        grid=(depth, 6),
        in_specs=[
            pl.BlockSpec((b, D_MODEL), lambda l, k: (0, 0)),
            pl.BlockSpec((None, D_MODEL, D_MODEL), lambda l, k: (l, 0, k)),
            pl.BlockSpec((None, None, 1, D_MODEL), lambda l, k: (l, k, 0, 0)),
        ],
        out_specs=pl.BlockSpec((None, None, b, D_MODEL), lambda l, k: (l, k, 0, 0)),
        out_shape=jax.ShapeDtypeStruct((depth, 6, b, D_MODEL), F32),
        compiler_params=pltpu.CompilerParams(
            dimension_semantics=("arbitrary", "arbitrary"), vmem_limit_bytes=VMEM_LIMIT),
        name="modulation",
    )(c, ada_w, ada_b.reshape(depth, 6, 1, D_MODEL))


def _rope_kernel(pos_ref, inv_ref, sgn_ref, cos_ref, sin_ref):
    ang = pos_ref[...].astype(F32) * inv_ref[...]
    cos_ref[...] = jnp.cos(ang)
    sin_ref[...] = jnp.sin(ang) * sgn_ref[...]


def _rope_tables(positions):
    n = positions.size
    tm = min(2048, n)
    half = ROPE_DIM // 2
    inv_freq = ROPE_THETA ** (-jnp.arange(half, dtype=F32) * 2.0 / ROPE_DIM)
    lane = np.arange(LANES) % HEAD_DIM
    inv_lane = jnp.where(lane < ROPE_DIM, inv_freq[lane % half], 0.0).reshape(1, LANES).astype(F32)
    sgn_lane = jnp.asarray(np.where(lane < half, -1.0, np.where(lane < ROPE_DIM, 1.0, 0.0)),
                           F32).reshape(1, LANES)
    return pl.pallas_call(
        _rope_kernel,
        grid=(n // tm,),
        in_specs=[
            pl.BlockSpec((tm, 1), lambda i: (i, 0)),
            pl.BlockSpec((1, LANES), lambda i: (0, 0)),
            pl.BlockSpec((1, LANES), lambda i: (0, 0)),
        ],
        out_specs=[pl.BlockSpec((tm, LANES), lambda i: (i, 0))] * 2,
        out_shape=[jax.ShapeDtypeStruct((n, LANES), F32)] * 2,
        compiler_params=pltpu.CompilerParams(dimension_semantics=("arbitrary",)),
        name="rope_tables",
    )(positions.reshape(n, 1), inv_lane, sgn_lane)


def _modulated_norm(x, gain, scale, shift):
    ms = jnp.mean(x * x, axis=-1, keepdims=True)
    return x * lax.rsqrt(ms + NORM_EPS) * gain * (1.0 + scale) + shift


def _inproj_kernel(x_ref, g_ref, sc_ref, sh_ref, w_ref, o_ref):
    h = _modulated_norm(x_ref[...], g_ref[...], sc_ref[...], sh_ref[...])
    o_ref[...] = _dot(h.astype(BF16), w_ref[...])


def _mod_spec(layer, which, tiles_per_batch):
    return pl.BlockSpec((None, None, None, 1, D_MODEL),
                        lambda i: (layer, which, i // tiles_per_batch, 0, 0))


def _inproj(x2, mod5, gain, w_pad, layer, seq):
    n = x2.shape[0]
    tm = min(512, seq)
    tpb = seq // tm
    return pl.pallas_call(
        _inproj_kernel,
        grid=(n // tm,),
        in_specs=[
            pl.BlockSpec((tm, D_MODEL), lambda i: (i, 0)),
            pl.BlockSpec((1, D_MODEL), lambda i: (0, 0)),
            _mod_spec(layer, 1, tpb),
            _mod_spec(layer, 0, tpb),
            pl.BlockSpec((None, D_MODEL, D_IN_PAD), lambda i: (layer, 0, 0), pipeline_mode=pl.Buffered(1)),
        ],
        out_specs=pl.BlockSpec((tm, D_IN_PAD), lambda i: (i, 0)),
        out_shape=jax.ShapeDtypeStruct((n, D_IN_PAD), F32),
        compiler_params=pltpu.CompilerParams(
            dimension_semantics=("arbitrary",), vmem_limit_bytes=VMEM_LIMIT),
        name="inproj",
    )(x2, gain, mod5, mod5, w_pad)


def _attn_consts():
    i = np.arange(LANES)
    within = i % HEAD_DIM
    half = ROPE_DIM // 2
    src = np.where(within < half, i + half, np.where(within < ROPE_DIM, i - half, -1))
    rot = (i[:, None] == src[None, :]).astype(np.float32)
    swap = (i[:, None] == ((i + HEAD_DIM) % LANES)[None, :]).astype(np.float32)
    return jnp.asarray(rot, BF16), jnp.asarray(swap, BF16)


def _attn_kernel(sink_ref, q_ref, k_ref, v_ref, cos_ref, sin_ref, qg_ref, kg_ref, bd_ref,
                 rot_ref, swap_ref, o_ref, kvar_ref, vvar_ref):
    n = pl.program_id(0)
    w = WINDOW
    nb = q_ref.shape[0]
    rows = nb * w

    @pl.when(n == 0)
    def _():
        kvar_ref[:, :, 0:w, :] = jnp.zeros((4, nb, w, LANES), BF16)
        vvar_ref[:, :, 0:w, :] = jnp.zeros((4, nb, w, LANES), BF16)

    @pl.when(n > 0)
    def _():
        kvar_ref[:, :, 0:w, :] = kvar_ref[:, :, w:2 * w, :]
        vvar_ref[:, :, 0:w, :] = vvar_ref[:, :, w:2 * w, :]

    cos = cos_ref[...].reshape(rows, LANES)
    sin = sin_ref[...].reshape(rows, LANES)
    bd = bd_ref[...]
    rot = rot_ref[...]
    swap = swap_ref[...]
    lo = lax.broadcasted_iota(jnp.int32, (rows, LANES), 1) < HEAD_DIM

    def norm_rope(xp, gain):
        y = _head_rmsnorm(xp, bd, gain)
        return y * cos + _dot(y.astype(BF16), rot) * sin

    def variants(x):
        xb = x.astype(BF16)
        xs = _dot(xb, swap).astype(BF16)
        zero = jnp.zeros_like(xb)
        return [jnp.where(lo, xb, zero), jnp.where(lo, zero, xs),
                jnp.where(lo, xs, zero), jnp.where(lo, zero, xb)]

    kn = norm_rope(k_ref[...].reshape(rows, LANES), kg_ref[...])
    for i, t in enumerate(variants(kn)):
        kvar_ref[i, :, w:2 * w, :] = t.reshape(nb, w, LANES)
    for i, t in enumerate(variants(v_ref[...].reshape(rows, LANES))):
        vvar_ref[i, :, w:2 * w, :] = t.reshape(nb, w, LANES)

    qi = lax.broadcasted_iota(jnp.int32, (1, w, 2 * w), 1)
    kj = lax.broadcasted_iota(jnp.int32, (1, w, 2 * w), 2)
    delta = qi + w - kj
    valid = (delta >= 0) & (delta < w) & ((n * w + kj - w) >= 0)
    bias = jnp.where(valid, 0.0, -jnp.inf).astype(F32)

    for p in range(A_Q_HEADS // 2):
        g = (2 * p) // (A_Q_HEADS // A_KV_HEADS)
        qn = norm_rope(q_ref[:, :, p * LANES:(p + 1) * LANES].reshape(rows, LANES), qg_ref[...])
        qb = (qn * (HEAD_DIM ** -0.5 * LOG2E)).astype(BF16).reshape(nb, w, LANES)
        out = None
        for par in range(2):
            sink = sink_ref[2 * p + par] * LOG2E
            s = _dot(qb, kvar_ref[2 * g + par], BNT) + bias
            m = jnp.maximum(jnp.max(s, axis=-1, keepdims=True), sink)
            e = jnp.exp2(s - m)
            den = jnp.sum(e, axis=-1, keepdims=True) + jnp.exp2(sink - m)
            o = _dot(e.astype(BF16), vvar_ref[2 * g + par], BNN) / den
            out = o if out is None else out + o
        o_ref[:, :, p * LANES:(p + 1) * LANES] = out.astype(o_ref.dtype)


def _attention(proj3, cos_t, sin_t, q_norm, k_norm, sinks, bd128):
    batch, seq, _ = proj3.shape
    nblk = seq // WINDOW
    qw = A_Q_HEADS * HEAD_DIM
    qg = jnp.tile(q_norm.astype(F32), 2).reshape(1, LANES)
    kg = jnp.tile(k_norm.astype(F32), 2).reshape(1, LANES)
    const = lambda shape: pl.BlockSpec(shape, lambda i: (0, 0))
    return pl.pallas_call(
        _attn_kernel,
        grid=(nblk,),
        in_specs=[
            pl.BlockSpec(memory_space=pltpu.SMEM),
            pl.BlockSpec((batch, WINDOW, qw), lambda i: (0, i, 0)),
            pl.BlockSpec((batch, WINDOW, LANES), lambda i: (0, i, qw // LANES)),
            pl.BlockSpec((batch, WINDOW, LANES), lambda i: (0, i, qw // LANES + 1)),
            pl.BlockSpec((batch, WINDOW, LANES), lambda i: (0, i, 0)),
            pl.BlockSpec((batch, WINDOW, LANES), lambda i: (0, i, 0)),
            const((1, LANES)), const((1, LANES)), const((LANES, LANES)),
            const((LANES, LANES)), const((LANES, LANES)),
        ],
        out_specs=pl.BlockSpec((batch, WINDOW, qw), lambda i: (0, i, 0)),
        out_shape=jax.ShapeDtypeStruct((batch, seq, qw), BF16),
        scratch_shapes=[pltpu.VMEM((4, batch, 2 * WINDOW, LANES), BF16),
                        pltpu.VMEM((4, batch, 2 * WINDOW, LANES), BF16)],
        compiler_params=pltpu.CompilerParams(
            dimension_semantics=("arbitrary",), vmem_limit_bytes=VMEM_LIMIT),
        name="attention",
    )(sinks.astype(F32), proj3, proj3, proj3, cos_t, sin_t, qg, kg, bd128, *_attn_consts())


def _hgrn_consts():
    c = CHUNK
    t = np.arange(c)[:, None]
    r = np.arange(c)[None, :]
    mall = np.concatenate([(r <= t), (r > t)], axis=0).astype(np.float32)
    s = np.arange(LANES)[None, :] % c
    x = t ^ s
    lvl = np.where(t > s, np.floor(np.log2(np.maximum(x, 1))).astype(np.int32),
                   np.where(t == s, -1, -2)).astype(np.int32)
    return jnp.asarray(mall, BF16), jnp.asarray(lvl, jnp.int32)


def _head_block_mask(nh):
    ri = lax.broadcasted_iota(jnp.int32, (nh, nh), 0) // HEAD_DIM
    ci = lax.broadcasted_iota(jnp.int32, (nh, nh), 1) // HEAD_DIM
    return ri == ci


def _hgrn_kernel(q_ref, z_ref, v_ref, gt_ref, lbl_ref, gain_ref, mall_ref, lvl_ref, bd_ref,
                 o_ref, st_ref, *, layer):
    c = CHUNK
    nh = B_HEADS * HEAD_DIM
    batch, span, _ = q_ref.shape
    nb = batch
    rows = nb * c

    @pl.when(pl.program_id(0) == 0)
    def _():
        st_ref[...] = jnp.zeros_like(st_ref)

    q = q_ref[...].reshape(rows, nh)
    z = z_ref[...].reshape(rows, nh)
    v = v_ref[...].reshape(rows, nh)

    log_sig = jnp.minimum(z, 0.0) - jnp.log1p(jnp.exp(-jnp.abs(z)))
    if layer == 0:
        lf = log_sig
        kk = jax.nn.sigmoid(-z)
    else:
        lg = lbl_ref[...]
        e = jnp.exp(lg - jnp.max(lg, axis=0, keepdims=True))
        sm = e / jnp.sum(e, axis=0, keepdims=True)
        lb = sm[1:2, :]
        for j in range(2, layer + 1):
            lb = lb + sm[j:j + 1, :]
        a = jnp.log(lb)
        b = jnp.log1p(-lb) + log_sig
        lf = jnp.maximum(a, b) + jnp.log1p(jnp.exp(-jnp.abs(a - b)))
        kk = (1.0 - lb) * jax.nn.sigmoid(-z)

    lf3 = lf.reshape(nb, c, nh)
    pieces = _split(lf3, 3)
    m_cr = jnp.broadcast_to(mall_ref[0:2 * c, :][None], (nb, 2 * c, c))
    cr = _dot(m_cr, pieces[0], BNN) + _dot(m_cr, pieces[1], BNN) + _dot(m_cr, pieces[2], BNN)
    cum = cr[:, 0:c]
    rest = cr[:, c:2 * c]

    trow_full = lax.broadcasted_iota(jnp.int32, (1, c, nh), 1)
    cum8 = cum.reshape(nb, c // SUBLANES, SUBLANES, nh)
    sub = lax.broadcasted_iota(jnp.int32, (1, 1, SUBLANES, nh), 2)
    dlv = [jnp.where((trow_full & 1) == 1, lf3, jnp.zeros_like(lf3))]
    for l in range(1, 6):
        m = 1 << l
        if 2 * m > SUBLANES:
            ref = jnp.concatenate(
                [jnp.broadcast_to(cum[:, j + m - 1:j + m, :], (nb, 2 * m, nh)) for j in range(0, c, 2 * m)],
                axis=1)
        elif 2 * m == SUBLANES:
            ref = jnp.broadcast_to(cum8[:, :, m - 1:m, :], cum8.shape).reshape(nb, c, nh)
        else:
            ref = jnp.where(sub < 2 * m, jnp.broadcast_to(cum8[:, :, m - 1:m, :], cum8.shape),
                            jnp.broadcast_to(cum8[:, :, 3 * m - 1:3 * m, :], cum8.shape)).reshape(nb, c, nh)
        diff = cum - ref
        dlv.append(jnp.where(((trow_full >> l) & 1) == 1, diff, -diff))

    q3 = q.reshape(nb, c, nh)
    k3 = kk.reshape(nb, c, nh)
    vb = v.reshape(nb, c, nh).astype(BF16)
    lvl = lvl_ref[...][None]
    trow = lax.broadcasted_iota(jnp.int32, (1, c, LANES), 1)
    o_intra = []
    for p in range(B_HEADS // 2):
        sl = slice(p * LANES, (p + 1) * LANES)
        qp, kp = q3[:, :, sl], k3[:, :, sl]
        pm = jnp.where(lvl == -1, _dot(qp.astype(BF16), _blockdiag_t(kp.astype(BF16)), BNN),
                       jnp.zeros((nb, c, LANES), F32))
        for l in range(6):
            is_q = ((trow >> l) & 1) == 1
            xl = (jnp.where(is_q, qp, kp) * jnp.exp(dlv[l][:, :, sl])).astype(BF16)
            pm = jnp.where(lvl == l, _dot(xl, _blockdiag_t(xl), BNN), pm)
        o_intra.append(_dot(pm.astype(BF16), _blockdiag(vb[:, :, sl]), BNN))
    o_intra = jnp.concatenate(o_intra, axis=2)

    q_dec = (q3 * jnp.exp(cum)).astype(BF16)
    k_dec = (k3 * jnp.exp(rest)).astype(BF16)
    dec_last = jnp.exp(cum[:, c - 1:c, :])
    blk = _head_block_mask(nh)[None]
    st = st_ref[...]
    st_t = jnp.swapaxes(st.astype(BF16), 1, 2)
    o_inter = _dot(q_dec, jnp.where(blk, st_t, jnp.zeros_like(st_t)), BNN)
    upd = _dot(jnp.swapaxes(vb, 1, 2), k_dec, BNN)
    st_ref[...] = st * dec_last + jnp.where(blk, upd, jnp.zeros_like(upd))
    o = (o_intra + o_inter).reshape(rows, nh)

    y = _head_rmsnorm(o, bd_ref[...], gain_ref[...])
    o_ref[...] = (y * _silu(gt_ref[...].reshape(rows, nh))).reshape(batch, span, nh).astype(o_ref.dtype)


def _hgrn(proj3, lb_logits, gain, bd256, layer):
    batch, seq, _ = proj3.shape
    span = CHUNK
    nh = B_HEADS * HEAD_DIM
    depth = lb_logits.shape[0]
    mall, lvl = _hgrn_consts()
    col0 = (A_Q_HEADS + 2 * A_KV_HEADS) * HEAD_DIM // nh
    const = lambda shape: pl.BlockSpec(shape, lambda i: (0, 0))
    return pl.pallas_call(
        functools.partial(_hgrn_kernel, layer=layer),
        grid=(seq // span,),
        in_specs=[pl.BlockSpec((batch, span, nh), lambda i, k=k: (0, i, col0 + k)) for k in range(4)]
        + [const((depth, nh)), const((1, nh)), const(mall.shape), const(lvl.shape), const((nh, nh))],
        out_specs=pl.BlockSpec((batch, span, nh), lambda i: (0, i, 0)),
        out_shape=jax.ShapeDtypeStruct((batch, seq, nh), BF16),
        scratch_shapes=[pltpu.VMEM((batch, nh, nh), F32)],
        compiler_params=pltpu.CompilerParams(
            dimension_semantics=("arbitrary",), vmem_limit_bytes=VMEM_LIMIT),
        name="hgrn2",
    )(proj3, proj3, proj3, proj3, lb_logits.astype(F32),
      jnp.tile(gain.astype(F32), B_HEADS).reshape(1, nh), mall, lvl, bd256)


def _gdn_consts():
    c = CHUNK
    t = np.arange(c)[:, None]
    r = np.arange(c)[None, :]
    tri2 = np.concatenate([(r <= t), (r > t)], axis=0).astype(np.float32)
    return jnp.asarray(tri2, BF16)


def _gdn_kernel(cq_ref, ck_ref, cv_ref, cg_ref, gate_ref, cw_ref, alog_ref, dt_ref, gain_ref,
                tri_ref, bd_ref, o_ref, xbuf_ref, s_ref):
    c = CHUNK
    nh = C_HEADS * HEAD_DIM
    batch, span, _ = cq_ref.shape
    nb = batch
    rows = nb * c
    tail = SUBLANES

    @pl.when(pl.program_id(0) == 0)
    def _():
        s_ref[...] = jnp.zeros_like(s_ref)
        xbuf_ref[:, 0:tail, :] = jnp.zeros((batch, tail, 3 * nh), F32)

    xbuf_ref[:, tail:tail + span, 0:nh] = cq_ref[...]
    xbuf_ref[:, tail:tail + span, nh:2 * nh] = ck_ref[...]
    xbuf_ref[:, tail:tail + span, 2 * nh:3 * nh] = cv_ref[...]
    w = cw_ref[...]
    groups = span // tail
    x9 = xbuf_ref[...].reshape(batch, groups + 1, tail, 3 * nh)
    sub = lax.broadcasted_iota(jnp.int32, (1, 1, tail, 3 * nh), 2)
    y = x9[:, 1:] * w[CONV_WIDTH - 1:CONV_WIDTH, :][None, None]
    for k in range(1, CONV_WIDTH):
        rk = pltpu.roll(x9, k, 2)
        shifted = jnp.where(sub >= k, rk[:, 1:], rk[:, :groups])
        y = y + shifted * w[CONV_WIDTH - 1 - k:CONV_WIDTH - k, :][None, None]
    xbuf_ref[:, 0:tail, :] = xbuf_ref[:, span:span + tail, :]
    y = _silu(y).reshape(rows, 3 * nh)
    q, k, v = y[:, 0:nh], y[:, nh:2 * nh], y[:, 2 * nh:3 * nh]

    bd = bd_ref[...]
    q = q * lax.rsqrt(_dot(jnp.square(q).astype(BF16), bd) + NORM_EPS) * (HEAD_DIM ** -0.5)
    k = k * lax.rsqrt(_dot(jnp.square(k).astype(BF16), bd) + NORM_EPS)

    gb = gate_ref[...].reshape(rows, LANES)
    head = lax.broadcasted_iota(jnp.int32, (rows, nh), 1) // HEAD_DIM

    def per_head(col0):
        out = jnp.broadcast_to(gb[:, col0:col0 + 1], (rows, nh))
        for h in range(1, C_HEADS):
            out = jnp.where(head == h, jnp.broadcast_to(gb[:, col0 + h:col0 + h + 1], (rows, nh)), out)
        return out

    beta = jax.nn.sigmoid(per_head(0))
    xg = per_head(C_HEADS) + dt_ref[...]
    g = -jnp.exp(alog_ref[...]) * (jnp.maximum(xg, 0.0) + jnp.log1p(jnp.exp(-jnp.abs(xg))))

    g3 = g.reshape(nb, c, nh)
    tri2 = jnp.broadcast_to(tri_ref[...][None], (nb, 2 * c, c))
    g_pieces = _split(g3, 3)
    gsum = None
    for piece in g_pieces:
        t = _dot(tri2, piece, BNN)
        gsum = t if gsum is None else gsum + t
    gcum = gsum[:, 0:c]
    grest = gsum[:, c:2 * c]
    exp_g = jnp.exp(gcum)
    q3 = q.reshape(nb, c, nh)
    k3 = k.reshape(nb, c, nh)
    beta3 = beta.reshape(nb, c, nh)
    kb = k3 * beta3
    vb = v.reshape(nb, c, nh) * beta3
    kbg = kb * exp_g

    ti = lax.broadcasted_iota(jnp.int32, (1, c, nh), 1)
    si = lax.broadcasted_iota(jnp.int32, (1, c, nh), 2) & (HEAD_DIM - 1)
    incl = ti >= si
    strict = ti > si
    zero = jnp.zeros((nb, c, nh), F32)
    eye = jnp.where(ti == si, 1.0, 0.0).astype(F32)
    same16 = (ti >> 4) == (si >> 4)

    def pmm(a, b):
        return _dot(a.astype(BF16), _blockdiag(b.astype(BF16)), BNN)

    def pmm2(a, y):
        yb = y.astype(BF16)
        rhs = jnp.concatenate([_blockdiag(yb[:, :, 0:nh]), _blockdiag(yb[:, :, nh:2 * nh])], axis=2)
        return _dot(a.astype(BF16), rhs, BNN)

    ldiff = None
    for piece in g_pieces[0:2]:
        t = _dot(tri2[:, 0:c], jnp.where(strict, piece, jnp.zeros_like(piece)), BNN)
        ldiff = t if ldiff is None else ldiff + t
    lmat = jnp.where(incl, jnp.exp(jnp.where(incl, ldiff, zero)), zero)
    kbd_t = _blockdiag_t(k3.astype(BF16))
    amat = jnp.where(strict, _dot(kb.astype(BF16), kbd_t, BNN) * lmat, zero)
    dmat = jnp.where(same16, amat, zero)
    noff = amat - dmat
    bm = -dmat
    b2 = pmm(bm, bm)
    b4 = pmm(b2, b2)
    b8 = pmm(b4, b4)
    td = eye + bm
    td = td + pmm(td, b2)
    td = td + pmm(td, b4)
    td = td + pmm(td, b8)
    mm = pmm(td, noff)
    y0 = pmm2(td, jnp.concatenate([vb, kbg], axis=2))
    m2 = pmm(mm, mm)
    y1 = y0 + pmm2(m2, y0)
    y2 = y1 - pmm2(mm, y1)
    u = y2[:, :, 0:nh]
    wmat = y2[:, :, nh:2 * nh].astype(BF16)
    qk = jnp.where(incl, _dot(q3.astype(BF16), kbd_t, BNN) * lmat, zero).astype(BF16)

    q_dec = (q3 * exp_g).astype(BF16)
    k_dec = (k3 * jnp.exp(grest)).astype(BF16)
    dec_last = jnp.exp(gcum[:, c - 1:c, :])
    blk = _head_block_mask(nh)[None]
    s = s_ref[...]
    sb = s.astype(BF16)
    v_new = (u - _dot(wmat, sb, BNN)).astype(BF16)
    o = _dot(q_dec, sb, BNN) + _dot(qk, _blockdiag(v_new), BNN)
    upd = _dot(jnp.swapaxes(k_dec, 1, 2), v_new, BNN)
    s_ref[...] = s * dec_last + jnp.where(blk, upd, jnp.zeros_like(upd))
    o = o.reshape(rows, nh)

    yo = _head_rmsnorm(o, bd, gain_ref[...])
    o_ref[...] = (yo * _silu(cg_ref[...].reshape(rows, nh))).reshape(batch, span, nh).astype(o_ref.dtype)


def _gdn(proj3, conv_w, a_log, dt_bias, gain, bd256):
    batch, seq, _ = proj3.shape
    span = CHUNK
    nh = C_HEADS * HEAD_DIM
    tri2 = _gdn_consts()
    col0 = ((A_Q_HEADS + 2 * A_KV_HEADS) * HEAD_DIM + 4 * nh) // nh
    const = lambda shape: pl.BlockSpec(shape, lambda i: (0, 0))
    rep = lambda a: jnp.repeat(a.astype(F32), HEAD_DIM).reshape(1, nh)
    return pl.pallas_call(
        _gdn_kernel,
        grid=(seq // span,),
        in_specs=[pl.BlockSpec((batch, span, nh), lambda i, k=k: (0, i, col0 + k)) for k in range(4)]
        + [pl.BlockSpec((batch, span, LANES), lambda i: (0, i, GATE_COL // LANES)),
           const((CONV_WIDTH, 3 * nh)), const((1, nh)), const((1, nh)), const((1, nh)),
           const(tri2.shape), const((nh, nh))],
        out_specs=pl.BlockSpec((batch, span, nh), lambda i: (0, i, 0)),
        out_shape=jax.ShapeDtypeStruct((batch, seq, nh), BF16),
        scratch_shapes=[pltpu.VMEM((batch, span + SUBLANES, 3 * nh), F32),
                        pltpu.VMEM((batch, nh, nh), F32)],
        compiler_params=pltpu.CompilerParams(
            dimension_semantics=("arbitrary",), vmem_limit_bytes=VMEM_LIMIT),
        name="gated_deltanet",
    )(proj3, proj3, proj3, proj3, proj3, conv_w.astype(F32), rep(a_log), rep(dt_bias),
      jnp.tile(gain.astype(F32), C_HEADS).reshape(1, nh), tri2, bd256)


def _outproj_ffn_kernel(a_ref, b_ref, c_ref, x_ref, gtm_ref, g_ref, sc_ref, sh_ref, gtf_ref,
                        wo_ref, wg_ref, wu_ref, wd_ref, o_ref):
    wa = A_Q_HEADS * HEAD_DIM
    wb = wa + B_HEADS * HEAD_DIM
    y = (_dot(a_ref[...], wo_ref[0:wa, :]) + _dot(b_ref[...], wo_ref[wa:wb, :])
         + _dot(c_ref[...], wo_ref[wb:D_MIX, :]))
    x1 = x_ref[...] + gtm_ref[...] * y
    h = _modulated_norm(x1, g_ref[...], sc_ref[...], sh_ref[...]).astype(BF16)
    act = (_silu(_dot(h, wg_ref[...])) * _dot(h, wu_ref[...])).astype(BF16)
    o_ref[...] = x1 + gtf_ref[...] * _dot(act, wd_ref[...])


def _outproj_ffn(out_a, out_b, out_c, x2, mod5, gain, w_out, w_gate, w_up, w_down, layer, seq):
    n = x2.shape[0]
    tm = min(512, seq)
    tpb = seq // tm
    tile = lambda width: pl.BlockSpec((tm, width), lambda i: (i, 0))
    resident = lambda rows, cols: pl.BlockSpec((None, rows, cols), lambda i: (layer, 0, 0),
                                               pipeline_mode=pl.Buffered(1))
    return pl.pallas_call(
        _outproj_ffn_kernel,
        grid=(n // tm,),
        in_specs=[tile(out_a.shape[1]), tile(out_b.shape[1]), tile(out_c.shape[1]), tile(D_MODEL),
                  _mod_spec(layer, 2, tpb), pl.BlockSpec((1, D_MODEL), lambda i: (0, 0)),
                  _mod_spec(layer, 4, tpb), _mod_spec(layer, 3, tpb), _mod_spec(layer, 5, tpb),
                  resident(D_MIX, D_MODEL), resident(D_MODEL, D_FF), resident(D_MODEL, D_FF),
                  resident(D_FF, D_MODEL)],
        out_specs=tile(D_MODEL),
        out_shape=jax.ShapeDtypeStruct((n, D_MODEL), F32),
        compiler_params=pltpu.CompilerParams(
            dimension_semantics=("arbitrary",), vmem_limit_bytes=VMEM_LIMIT),
        name="outproj_ffn",
    )(out_a, out_b, out_c, x2, mod5, gain, mod5, mod5, mod5, w_out, w_gate, w_up, w_down)


def _blockdiag_ones(size):
    i = np.arange(size) // HEAD_DIM
    return jnp.asarray((i[:, None] == i[None, :]).astype(np.float32), BF16)


def kernel(x, c, positions, ada_w, ada_b, norm_mix, w_in, attn_q_norm, attn_k_norm, attn_sinks,
           hgrn_lb_logits, hgrn_out_norm, gdn_conv_w, gdn_a_log, gdn_dt_bias, gdn_out_norm, w_out,
           norm_ffn, w_gate, w_up, w_down):
    batch, seq, _ = x.shape
    depth = ada_w.shape[0]
    n = batch * seq
    x2 = x.reshape(n, D_MODEL).astype(F32)

    mod = _modulation(c.astype(F32), ada_w.astype(F32), ada_b.astype(F32))
    mod5 = mod.reshape(depth, 6, batch, 1, D_MODEL)
    cos_t, sin_t = _rope_tables(positions)
    cos3 = cos_t.reshape(batch, seq, LANES)
    sin3 = sin_t.reshape(batch, seq, LANES)
    bd128 = _blockdiag_ones(LANES)
    bd256 = _blockdiag_ones(B_HEADS * HEAD_DIM)

    w_pad = jnp.pad(w_in.astype(BF16), ((0, 0), (0, 0), (0, D_IN_PAD - D_IN)))
    w_out, w_gate, w_up, w_down = (w.astype(BF16) for w in (w_out, w_gate, w_up, w_down))

    for l in range(depth):
        proj = _inproj(x2, mod5, norm_mix[l].astype(F32).reshape(1, D_MODEL), w_pad, l, seq)
        proj3 = proj.reshape(batch, seq, D_IN_PAD)
        out_a = _attention(proj3, cos3, sin3, attn_q_norm[l], attn_k_norm[l], attn_sinks[l],
                           bd128).reshape(n, -1)
        out_b = _hgrn(proj3, hgrn_lb_logits, hgrn_out_norm[l], bd256, l).reshape(n, -1)
        out_c = _gdn(proj3, gdn_conv_w[l], gdn_a_log[l], gdn_dt_bias[l], gdn_out_norm[l],
                     bd256).reshape(n, -1)
        x2 = _outproj_ffn(out_a, out_b, out_c, x2, mod5, norm_ffn[l].astype(F32).reshape(1, D_MODEL),
                          w_out, w_gate, w_up, w_down, l, seq)
    return x2.reshape(batch, seq, D_MODEL).astype(x.dtype)
```

```python
import functools
import math

import numpy as np
import jax
import jax.numpy as jnp
from jax import lax
from jax.experimental import pallas as pl
from jax.experimental.pallas import tpu as pltpu

F32 = jnp.float32
BF16 = jnp.bfloat16

D_MODEL = 1024
HEAD_DIM = 64
A_Q_HEADS = 8
A_KV_HEADS = 2
WINDOW = 128
ROPE_DIM = HEAD_DIM // 4
ROPE_THETA = 500000.0
B_HEADS = 4
C_HEADS = 4
CONV_WIDTH = 4
CHUNK = 64
D_FF = 2816
D_MIX = 1024
D_IN = 2824
NORM_EPS = 1e-6
LOG2E = math.log2(math.e)

LANES = 128
SUBLANES = 8
D_IN_PAD = 2944
GATE_COL = 2816
VMEM_LIMIT = 56 * 1024 * 1024

NN = (((1,), (0,)), ((), ()))
NT = (((1,), (1,)), ((), ()))
TN = (((0,), (0,)), ((), ()))
BNN = (((2,), (1,)), ((0,), (0,)))
BNT = (((2,), (2,)), ((0,), (0,)))


def _dot(a, b, dims=NN):
    return lax.dot_general(a, b, dims, preferred_element_type=F32)


def _split(x, n):
    parts = []
    r = x
    for i in range(n):
        p = r.astype(BF16)
        parts.append(p)
        if i + 1 < n:
            r = r - p.astype(F32)
    return parts


def _dot_exact_r(x, m, n=3):
    out = None
    for p in _split(x, n):
        t = _dot(p, m)
        out = t if out is None else out + t
    return out


def _dot3(a, b, dims=NN):
    a1, a2 = _split(a, 2)
    b1, b2 = _split(b, 2)
    return _dot(a1, b1, dims) + _dot(a1, b2, dims) + _dot(a2, b1, dims)


def _silu(x):
    return x * jax.nn.sigmoid(x)


def _lane_lo(shape):
    return (lax.broadcasted_iota(jnp.int32, shape, len(shape) - 1) & (LANES - 1)) < HEAD_DIM


def _blockdiag2(x):
    lo = _lane_lo(x.shape)
    zero = jnp.zeros_like(x)
    return jnp.concatenate([jnp.where(lo, x, zero), jnp.where(lo, zero, x)], axis=x.ndim - 2)


def _blockdiag(x):
    heads = x.shape[-1] // HEAD_DIM
    lane_head = lax.broadcasted_iota(jnp.int32, x.shape, x.ndim - 1) // HEAD_DIM
    zero = jnp.zeros_like(x)
    return jnp.concatenate([jnp.where(lane_head == i, x, zero) for i in range(heads)], axis=x.ndim - 2)


def _blockdiag_t(x):
    heads = x.shape[-1] // HEAD_DIM
    r = x.shape[1]
    xt = jnp.swapaxes(jnp.concatenate([x] * heads, axis=1), 1, 2)
    row_head = lax.broadcasted_iota(jnp.int32, xt.shape, 1) // HEAD_DIM
    col_head = lax.broadcasted_iota(jnp.int32, xt.shape, 2) // r
    return jnp.where(row_head == col_head, xt, jnp.zeros_like(xt))


def _head_rmsnorm(o, bd, gain):
    return o * lax.rsqrt(_dot(jnp.square(o).astype(BF16), bd) + NORM_EPS) * gain


def _log1p_exp_neg_abs(x):
    return jnp.log(1.0 + jnp.exp(-jnp.abs(x)))


def _mod_kernel(c_ref, w_ref, b_ref, o_ref):
    c = c_ref[...]
    o_ref[...] = _dot3(_silu(c), w_ref[...]) + b_ref[...]


def _modulation(c, ada_w, ada_b):
    depth = ada_w.shape[0]
    b = c.shape[0]
    return pl.pallas_call(
        _mod_kernel,
        grid=(depth, 6),
        in_specs=[
            pl.BlockSpec((b, D_MODEL), lambda l, k: (0, 0)),
            pl.BlockSpec((None, D_MODEL, D_MODEL), lambda l, k: (l, 0, k)),
            pl.BlockSpec((None, None, 1, D_MODEL), lambda l, k: (l, k, 0, 0)),
        ],
        out_specs=pl.BlockSpec((None, None, b, D_MODEL), lambda l, k: (l, k, 0, 0)),
        out_shape=jax.ShapeDtypeStruct((depth, 6, b, D_MODEL), F32),
        compiler_params=pltpu.CompilerParams(
            dimension_semantics=("arbitrary", "arbitrary"), vmem_limit_bytes=VMEM_LIMIT),
        name="modulation",
    )(c, ada_w, ada_b.reshape(depth, 6, 1, D_MODEL))


def _rope_kernel(pos_ref, inv_ref, sgn_ref, cos_ref, sin_ref):
    ang = pos_ref[...].astype(F32) * inv_ref[...]
    cos_ref[...] = jnp.cos(ang)
    sin_ref[...] = jnp.sin(ang) * sgn_ref[...]


def _rope_tables(positions):
    n = positions.size
    tm = min(2048, n)
    half = ROPE_DIM // 2
    inv_freq = ROPE_THETA ** (-jnp.arange(half, dtype=F32) * 2.0 / ROPE_DIM)
    lane = np.arange(LANES) % HEAD_DIM
    inv_lane = jnp.where(lane < ROPE_DIM, inv_freq[lane % half], 0.0).reshape(1, LANES).astype(F32)
    sgn_lane = jnp.asarray(np.where(lane < half, -1.0, np.where(lane < ROPE_DIM, 1.0, 0.0)),
                           F32).reshape(1, LANES)
    return pl.pallas_call(
        _rope_kernel,
        grid=(n // tm,),
        in_specs=[
            pl.BlockSpec((tm, 1), lambda i: (i, 0)),
            pl.BlockSpec((1, LANES), lambda i: (0, 0)),
            pl.BlockSpec((1, LANES), lambda i: (0, 0)),
        ],
        out_specs=[pl.BlockSpec((tm, LANES), lambda i: (i, 0))] * 2,
        out_shape=[jax.ShapeDtypeStruct((n, LANES), F32)] * 2,
        compiler_params=pltpu.CompilerParams(dimension_semantics=("arbitrary",)),
        name="rope_tables",
    )(positions.reshape(n, 1), inv_lane, sgn_lane)


def _modulated_norm(x, gain, scale, shift):
    ms = jnp.mean(x * x, axis=-1, keepdims=True)
    return x * lax.rsqrt(ms + NORM_EPS) * gain * (1.0 + scale) + shift


def _inproj_kernel(x_ref, g_ref, sc_ref, sh_ref, w_ref, o_ref):
    h = _modulated_norm(x_ref[...], g_ref[...], sc_ref[...], sh_ref[...])
    o_ref[...] = _dot(h.astype(BF16), w_ref[...])


def _mod_spec(layer, which, tiles_per_batch):
    return pl.BlockSpec((None, None, None, 1, D_MODEL),
                        lambda i: (layer, which, i // tiles_per_batch, 0, 0))


def _inproj(x2, mod5, gain, w_pad, layer, seq):
    n = x2.shape[0]
    tm = min(512, seq)
    tpb = seq // tm
    return pl.pallas_call(
        _inproj_kernel,
        grid=(n // tm,),
        in_specs=[
            pl.BlockSpec((tm, D_MODEL), lambda i: (i, 0)),
            pl.BlockSpec((1, D_MODEL), lambda i: (0, 0)),
            _mod_spec(layer, 1, tpb),
            _mod_spec(layer, 0, tpb),
            pl.BlockSpec((None, D_MODEL, D_IN_PAD), lambda i: (layer, 0, 0), pipeline_mode=pl.Buffered(1)),
        ],
        out_specs=pl.BlockSpec((tm, D_IN_PAD), lambda i: (i, 0)),
        out_shape=jax.ShapeDtypeStruct((n, D_IN_PAD), F32),
        compiler_params=pltpu.CompilerParams(
            dimension_semantics=("arbitrary",), vmem_limit_bytes=VMEM_LIMIT),
        name="inproj",
    )(x2, gain, mod5, mod5, w_pad)


def _attn_consts():
    i = np.arange(LANES)
    within = i % HEAD_DIM
    half = ROPE_DIM // 2
    src = np.where(within < half, i + half, np.where(within < ROPE_DIM, i - half, -1))
    rot = (i[:, None] == src[None, :]).astype(np.float32)
    swap = (i[:, None] == ((i + HEAD_DIM) % LANES)[None, :]).astype(np.float32)
    return jnp.asarray(rot, BF16), jnp.asarray(swap, BF16)


def _attn_kernel(sink_ref, q_ref, k_ref, v_ref, cos_ref, sin_ref, qg_ref, kg_ref, bd_ref,
                 rot_ref, swap_ref, o_ref, kvar_ref, vvar_ref):
    n = pl.program_id(0)
    w = WINDOW
    nb = q_ref.shape[0]
    rows = nb * w

    @pl.when(n == 0)
    def _():
        kvar_ref[:, :, 0:w, :] = jnp.zeros((4, nb, w, LANES), BF16)
        vvar_ref[:, :, 0:w, :] = jnp.zeros((4, nb, w, LANES), BF16)

    @pl.when(n > 0)
    def _():
        kvar_ref[:, :, 0:w, :] = kvar_ref[:, :, w:2 * w, :]
        vvar_ref[:, :, 0:w, :] = vvar_ref[:, :, w:2 * w, :]

    cos = cos_ref[...].reshape(rows, LANES)
    sin = sin_ref[...].reshape(rows, LANES)
    bd = bd_ref[...]
    rot = rot_ref[...]
    swap = swap_ref[...]
    lo = lax.broadcasted_iota(jnp.int32, (rows, LANES), 1) < HEAD_DIM

    def norm_rope(xp, gain):
        y = _head_rmsnorm(xp, bd, gain)
        return y * cos + _dot(y.astype(BF16), rot) * sin

    def variants(x):
        xb = x.astype(BF16)
        xs = _dot(xb, swap).astype(BF16)
        zero = jnp.zeros_like(xb)
        return [jnp.where(lo, xb, zero), jnp.where(lo, zero, xs),
                jnp.where(lo, xs, zero), jnp.where(lo, zero, xb)]

    kn = norm_rope(k_ref[...].reshape(rows, LANES), kg_ref[...])
    for i, t in enumerate(variants(kn)):
        kvar_ref[i, :, w:2 * w, :] = t.reshape(nb, w, LANES)
    for i, t in enumerate(variants(v_ref[...].reshape(rows, LANES))):
        vvar_ref[i, :, w:2 * w, :] = t.reshape(nb, w, LANES)

    qi = lax.broadcasted_iota(jnp.int32, (1, w, 2 * w), 1)
    kj = lax.broadcasted_iota(jnp.int32, (1, w, 2 * w), 2)
    delta = qi + w - kj
    valid = (delta >= 0) & (delta < w) & ((n * w + kj - w) >= 0)
    bias = jnp.where(valid, 0.0, -jnp.inf).astype(F32)
    lo3 = lax.broadcasted_iota(jnp.int32, (1, w, LANES), 2) < HEAD_DIM

    for p in range(A_Q_HEADS // 2):
        g = (2 * p) // (A_Q_HEADS // A_KV_HEADS)
        qn = norm_rope(q_ref[:, :, p * LANES:(p + 1) * LANES].reshape(rows, LANES), qg_ref[...])
        qb = (qn * (HEAD_DIM ** -0.5 * LOG2E)).astype(BF16).reshape(nb, w, LANES)
        num, ms, sums = None, [], []
        for par in range(2):
            sink = sink_ref[2 * p + par] * LOG2E
            s = _dot(qb, kvar_ref[2 * g + par], BNT) + bias
            m = jnp.maximum(jnp.max(s, axis=-1, keepdims=True), sink)
            e = jnp.exp2(s - m)
            ms.append(m)
            sums.append(jnp.sum(e, axis=-1, keepdims=True))
            pv = _dot(e.astype(BF16), vvar_ref[2 * g + par], BNN)
            num = pv if num is None else num + pv
        sink_pair = jnp.where(lo3, sink_ref[2 * p] * LOG2E, sink_ref[2 * p + 1] * LOG2E)
        den = jnp.where(lo3, sums[0], sums[1]) + jnp.exp2(sink_pair - jnp.where(lo3, ms[0], ms[1]))
        o_ref[:, :, p * LANES:(p + 1) * LANES] = (num / den).astype(o_ref.dtype)


def _attention(proj3, cos_t, sin_t, q_norm, k_norm, sinks, bd128):
    batch, seq, _ = proj3.shape
    nblk = seq // WINDOW
    qw = A_Q_HEADS * HEAD_DIM
    qg = jnp.tile(q_norm.astype(F32), 2).reshape(1, LANES)
    kg = jnp.tile(k_norm.astype(F32), 2).reshape(1, LANES)
    const = lambda shape: pl.BlockSpec(shape, lambda i: (0, 0))
    return pl.pallas_call(
        _attn_kernel,
        grid=(nblk,),
        in_specs=[
            pl.BlockSpec(memory_space=pltpu.SMEM),
            pl.BlockSpec((batch, WINDOW, qw), lambda i: (0, i, 0)),
            pl.BlockSpec((batch, WINDOW, LANES), lambda i: (0, i, qw // LANES)),
            pl.BlockSpec((batch, WINDOW, LANES), lambda i: (0, i, qw // LANES + 1)),
            pl.BlockSpec((batch, WINDOW, LANES), lambda i: (0, i, 0)),
            pl.BlockSpec((batch, WINDOW, LANES), lambda i: (0, i, 0)),
            const((1, LANES)), const((1, LANES)), const((LANES, LANES)),
            const((LANES, LANES)), const((LANES, LANES)),
        ],
        out_specs=pl.BlockSpec((batch, WINDOW, qw), lambda i: (0, i, 0)),
        out_shape=jax.ShapeDtypeStruct((batch, seq, qw), BF16),
        scratch_shapes=[pltpu.VMEM((4, batch, 2 * WINDOW, LANES), BF16),
                        pltpu.VMEM((4, batch, 2 * WINDOW, LANES), BF16)],
        compiler_params=pltpu.CompilerParams(
            dimension_semantics=("arbitrary",), vmem_limit_bytes=VMEM_LIMIT),
        name="attention",
    )(sinks.astype(F32), proj3, proj3, proj3, cos_t, sin_t, qg, kg, bd128, *_attn_consts())


def _hgrn_consts():
    c = CHUNK
    t = np.arange(c)[:, None]
    r = np.arange(c)[None, :]
    mall = np.concatenate([(r <= t), (r > t)], axis=0).astype(np.float32)
    s = np.arange(LANES)[None, :] % c
    x = t ^ s
    lvl = np.where(t > s, np.floor(np.log2(np.maximum(x, 1))).astype(np.int32),
                   np.where(t == s, -1, -2)).astype(np.int32)
    return jnp.asarray(mall, BF16), jnp.asarray(lvl, jnp.int32)


def _head_block_mask(nh):
    ri = lax.broadcasted_iota(jnp.int32, (nh, nh), 0) // HEAD_DIM
    ci = lax.broadcasted_iota(jnp.int32, (nh, nh), 1) // HEAD_DIM
    return ri == ci


def _hgrn_kernel(q_ref, z_ref, v_ref, gt_ref, lbl_ref, gain_ref, mall_ref, lvl_ref, bd_ref,
                 o_ref, st_ref, *, layer):
    c = CHUNK
    nh = B_HEADS * HEAD_DIM
    batch, span, _ = q_ref.shape
    nb = batch
    rows = nb * c

    @pl.when(pl.program_id(0) == 0)
    def _():
        st_ref[...] = jnp.zeros_like(st_ref)

    q = q_ref[...].reshape(rows, nh)
    z = z_ref[...].reshape(rows, nh)
    v = v_ref[...].reshape(rows, nh)

    log_sig = jnp.minimum(z, 0.0) - _log1p_exp_neg_abs(z)
    if layer == 0:
        lf = log_sig
        kk = jax.nn.sigmoid(-z)
    else:
        lg = lbl_ref[...]
        e = jnp.exp(lg - jnp.max(lg, axis=0, keepdims=True))
        sm = e / jnp.sum(e, axis=0, keepdims=True)
        lb = sm[1:2, :]
        for j in range(2, layer + 1):
            lb = lb + sm[j:j + 1, :]
        a = jnp.log(lb)
        b = jnp.log1p(-lb) + log_sig
        lf = jnp.maximum(a, b) + _log1p_exp_neg_abs(a - b)
        kk = (1.0 - lb) * jax.nn.sigmoid(-z)

    lf3 = lf.reshape(nb, c, nh)
    pieces = _split(lf3, 3)
    m_cr = jnp.broadcast_to(mall_ref[0:2 * c, :][None], (nb, 2 * c, c))
    cr = _dot(m_cr, pieces[0], BNN) + _dot(m_cr, pieces[1], BNN) + _dot(m_cr, pieces[2], BNN)
    cum = cr[:, 0:c]
    rest = cr[:, c:2 * c]

    trow_full = lax.broadcasted_iota(jnp.int32, (1, c, nh), 1)
    cum8 = cum.reshape(nb, c // SUBLANES, SUBLANES, nh)
    sub = lax.broadcasted_iota(jnp.int32, (1, 1, SUBLANES, nh), 2)
    dlv = [jnp.where((trow_full & 1) == 1, lf3, jnp.zeros_like(lf3))]
    for l in range(1, 6):
        m = 1 << l
        if 2 * m > SUBLANES:
            ref = jnp.concatenate(
                [jnp.broadcast_to(cum[:, j + m - 1:j + m, :], (nb, 2 * m, nh)) for j in range(0, c, 2 * m)],
                axis=1)
        elif 2 * m == SUBLANES:
            ref = jnp.broadcast_to(cum8[:, :, m - 1:m, :], cum8.shape).reshape(nb, c, nh)
        else:
            ref = jnp.where(sub < 2 * m, jnp.broadcast_to(cum8[:, :, m - 1:m, :], cum8.shape),
                            jnp.broadcast_to(cum8[:, :, 3 * m - 1:3 * m, :], cum8.shape)).reshape(nb, c, nh)
        diff = cum - ref
        dlv.append(jnp.where(((trow_full >> l) & 1) == 1, diff, -diff))

    q3 = q.reshape(nb, c, nh)
    k3 = kk.reshape(nb, c, nh)
    vb = v.reshape(nb, c, nh).astype(BF16)
    lvl = lvl_ref[...][None]
    trow = lax.broadcasted_iota(jnp.int32, (1, c, LANES), 1)
    o_intra = []
    for p in range(B_HEADS // 2):
        sl = slice(p * LANES, (p + 1) * LANES)
        qp, kp = q3[:, :, sl], k3[:, :, sl]
        pm = jnp.where(lvl == -1, _dot(qp.astype(BF16), _blockdiag_t(kp.astype(BF16)), BNN),
                       jnp.zeros((nb, c, LANES), F32))
        for l in range(6):
            is_q = ((trow >> l) & 1) == 1
            xl = (jnp.where(is_q, qp, kp) * jnp.exp(dlv[l][:, :, sl])).astype(BF16)
            pm = jnp.where(lvl == l, _dot(xl, _blockdiag_t(xl), BNN), pm)
        o_intra.append(_dot(pm.astype(BF16), _blockdiag(vb[:, :, sl]), BNN))
    o_intra = jnp.concatenate(o_intra, axis=2)

    q_dec = (q3 * jnp.exp(cum)).astype(BF16)
    k_dec = (k3 * jnp.exp(rest)).astype(BF16)
    dec_last = jnp.exp(cum[:, c - 1:c, :])
    blk = _head_block_mask(nh)[None]
    st = st_ref[...]
    st_t = jnp.swapaxes(st.astype(BF16), 1, 2)
    o_inter = _dot(q_dec, jnp.where(blk, st_t, jnp.zeros_like(st_t)), BNN)
    upd = _dot(jnp.swapaxes(vb, 1, 2), k_dec, BNN)
    st_ref[...] = st * dec_last + jnp.where(blk, upd, jnp.zeros_like(upd))
    o = (o_intra + o_inter).reshape(rows, nh)

    y = _head_rmsnorm(o, bd_ref[...], gain_ref[...])
    o_ref[...] = (y * _silu(gt_ref[...].reshape(rows, nh))).reshape(batch, span, nh).astype(o_ref.dtype)


def _hgrn(proj3, lb_logits, gain, bd256, layer):
    batch, seq, _ = proj3.shape
    span = CHUNK
    nh = B_HEADS * HEAD_DIM
    depth = lb_logits.shape[0]
    mall, lvl = _hgrn_consts()
    col0 = (A_Q_HEADS + 2 * A_KV_HEADS) * HEAD_DIM // nh
    const = lambda shape: pl.BlockSpec(shape, lambda i: (0, 0))
    return pl.pallas_call(
        functools.partial(_hgrn_kernel, layer=layer),
        grid=(seq // span,),
        in_specs=[pl.BlockSpec((batch, span, nh), lambda i, k=k: (0, i, col0 + k)) for k in range(4)]
        + [const((depth, nh)), const((1, nh)), const(mall.shape), const(lvl.shape), const((nh, nh))],
        out_specs=pl.BlockSpec((batch, span, nh), lambda i: (0, i, 0)),
        out_shape=jax.ShapeDtypeStruct((batch, seq, nh), BF16),
        scratch_shapes=[pltpu.VMEM((batch, nh, nh), F32)],
        compiler_params=pltpu.CompilerParams(
            dimension_semantics=("arbitrary",), vmem_limit_bytes=VMEM_LIMIT),
        name="hgrn2",
    )(proj3, proj3, proj3, proj3, lb_logits.astype(F32),
      jnp.tile(gain.astype(F32), B_HEADS).reshape(1, nh), mall, lvl, bd256)


def _gdn_consts():
    c = CHUNK
    t = np.arange(c)[:, None]
    r = np.arange(c)[None, :]
    tri2 = np.concatenate([(r <= t), (r > t)], axis=0).astype(np.float32)
    return jnp.asarray(tri2, BF16)


def _gdn_kernel(cq_ref, ck_ref, cv_ref, cg_ref, gate_ref, cw_ref, alog_ref, dt_ref, gain_ref,
                tri_ref, bd_ref, o_ref, xbuf_ref, s_ref):
    c = CHUNK
    nh = C_HEADS * HEAD_DIM
    batch, span, _ = cq_ref.shape
    nb = batch
    rows = nb * c
    tail = SUBLANES

    @pl.when(pl.program_id(0) == 0)
    def _():
        s_ref[...] = jnp.zeros_like(s_ref)
        xbuf_ref[:, 0:tail, :] = jnp.zeros((batch, tail, 3 * nh), F32)

    xbuf_ref[:, tail:tail + span, 0:nh] = cq_ref[...]
    xbuf_ref[:, tail:tail + span, nh:2 * nh] = ck_ref[...]
    xbuf_ref[:, tail:tail + span, 2 * nh:3 * nh] = cv_ref[...]
    w = cw_ref[...]
    groups = span // tail
    x9 = xbuf_ref[...].reshape(batch, groups + 1, tail, 3 * nh)
    sub = lax.broadcasted_iota(jnp.int32, (1, 1, tail, 3 * nh), 2)
    y = x9[:, 1:] * w[CONV_WIDTH - 1:CONV_WIDTH, :][None, None]
    for k in range(1, CONV_WIDTH):
        rk = pltpu.roll(x9, k, 2)
        shifted = jnp.where(sub >= k, rk[:, 1:], rk[:, :groups])
        y = y + shifted * w[CONV_WIDTH - 1 - k:CONV_WIDTH - k, :][None, None]
    xbuf_ref[:, 0:tail, :] = xbuf_ref[:, span:span + tail, :]
    y = _silu(y).reshape(rows, 3 * nh)
    q, k, v = y[:, 0:nh], y[:, nh:2 * nh], y[:, 2 * nh:3 * nh]

    bd = bd_ref[...]
    q = q * (lax.rsqrt(_dot(jnp.square(q).astype(BF16), bd) + NORM_EPS / HEAD_DIM) * (1.0 / HEAD_DIM))
    k = k * (lax.rsqrt(_dot(jnp.square(k).astype(BF16), bd) + NORM_EPS / HEAD_DIM) * (HEAD_DIM ** -0.5))

    gb = gate_ref[...].reshape(rows, LANES)
    head = lax.broadcasted_iota(jnp.int32, (rows, nh), 1) // HEAD_DIM

    def per_head(col0):
        out = jnp.broadcast_to(gb[:, col0:col0 + 1], (rows, nh))
        for h in range(1, C_HEADS):
            out = jnp.where(head == h, jnp.broadcast_to(gb[:, col0 + h:col0 + h + 1], (rows, nh)), out)
        return out

    beta = jax.nn.sigmoid(per_head(0))
    xg = per_head(C_HEADS) + dt_ref[...]
    g = -jnp.exp(alog_ref[...]) * (jnp.maximum(xg, 0.0) + _log1p_exp_neg_abs(xg))

    g3 = g.reshape(nb, c, nh)
    tri2 = jnp.broadcast_to(tri_ref[...][None], (nb, 2 * c, c))
    g_pieces = _split(g3, 3)
    gsum = None
    for piece in g_pieces:
        t = _dot(tri2, piece, BNN)
        gsum = t if gsum is None else gsum + t
    gcum = gsum[:, 0:c]
    grest = gsum[:, c:2 * c]
    exp_g = jnp.exp(gcum)
    q3 = q.reshape(nb, c, nh)
    k3 = k.reshape(nb, c, nh)
    beta3 = beta.reshape(nb, c, nh)
    kb = k3 * beta3
    vb = v.reshape(nb, c, nh) * beta3
    kbg = kb * exp_g

    ti = lax.broadcasted_iota(jnp.int32, (1, c, nh), 1)
    si = lax.broadcasted_iota(jnp.int32, (1, c, nh), 2) & (HEAD_DIM - 1)
    incl = ti >= si
    strict = ti > si
    zero = jnp.zeros((nb, c, nh), F32)
    eye = jnp.where(ti == si, 1.0, 0.0).astype(F32)
    same16 = (ti >> 4) == (si >> 4)

    def pmm(a, b):
        return _dot(a.astype(BF16), _blockdiag(b.astype(BF16)), BNN)

    def pmm2(a, y):
        yb = y.astype(BF16)
        rhs = jnp.concatenate([_blockdiag(yb[:, :, 0:nh]), _blockdiag(yb[:, :, nh:2 * nh])], axis=2)
        return _dot(a.astype(BF16), rhs, BNN)

    ldiff = None
    for piece in g_pieces[0:2]:
        t = _dot(tri2[:, 0:c], jnp.where(strict, piece, jnp.zeros_like(piece)), BNN)
        ldiff = t if ldiff is None else ldiff + t
    lmat = jnp.where(incl, jnp.exp(jnp.where(incl, ldiff, zero)), zero)
    kbd_t = _blockdiag_t(k3.astype(BF16))
    amat = jnp.where(strict, _dot(kb.astype(BF16), kbd_t, BNN) * lmat, zero)
    dmat = jnp.where(same16, amat, zero)
    noff = amat - dmat
    bm = -dmat
    b2 = pmm(bm, bm)
    b4 = pmm(b2, b2)
    b8 = pmm(b4, b4)
    td = eye + bm
    td = td + pmm(td, b2)
    td = td + pmm(td, b4)
    td = td + pmm(td, b8)
    mm = pmm(td, noff)
    y0 = pmm2(td, jnp.concatenate([vb, kbg], axis=2))
    m2 = pmm(mm, mm)
    y1 = y0 + pmm2(m2, y0)
    y2 = y1 - pmm2(mm, y1)
    u = y2[:, :, 0:nh]
    wmat = y2[:, :, nh:2 * nh].astype(BF16)
    qk = jnp.where(incl, _dot(q3.astype(BF16), kbd_t, BNN) * lmat, zero).astype(BF16)

    q_dec = (q3 * exp_g).astype(BF16)
    k_dec = (k3 * jnp.exp(grest)).astype(BF16)
    dec_last = jnp.exp(gcum[:, c - 1:c, :])
    blk = _head_block_mask(nh)[None]
    s = s_ref[...]
    sb = s.astype(BF16)
    v_new = (u - _dot(wmat, sb, BNN)).astype(BF16)
    o = _dot(q_dec, sb, BNN) + _dot(qk, _blockdiag(v_new), BNN)
    upd = _dot(jnp.swapaxes(k_dec, 1, 2), v_new, BNN)
    s_ref[...] = s * dec_last + jnp.where(blk, upd, jnp.zeros_like(upd))
    o = o.reshape(rows, nh)

    yo = _head_rmsnorm(o, bd, gain_ref[...])
    o_ref[...] = (yo * _silu(cg_ref[...].reshape(rows, nh))).reshape(batch, span, nh).astype(o_ref.dtype)


def _gdn(proj3, conv_w, a_log, dt_bias, gain, bd256):
    batch, seq, _ = proj3.shape
    span = CHUNK
    nh = C_HEADS * HEAD_DIM
    tri2 = _gdn_consts()
    col0 = ((A_Q_HEADS + 2 * A_KV_HEADS) * HEAD_DIM + 4 * nh) // nh
    const = lambda shape: pl.BlockSpec(shape, lambda i: (0, 0))
    rep = lambda a: jnp.repeat(a.astype(F32), HEAD_DIM).reshape(1, nh)
    return pl.pallas_call(
        _gdn_kernel,
        grid=(seq // span,),
        in_specs=[pl.BlockSpec((batch, span, nh), lambda i, k=k: (0, i, col0 + k)) for k in range(4)]
        + [pl.BlockSpec((batch, span, LANES), lambda i: (0, i, GATE_COL // LANES)),
           const((CONV_WIDTH, 3 * nh)), const((1, nh)), const((1, nh)), const((1, nh)),
           const(tri2.shape), const((nh, nh))],
        out_specs=pl.BlockSpec((batch, span, nh), lambda i: (0, i, 0)),
        out_shape=jax.ShapeDtypeStruct((batch, seq, nh), BF16),
        scratch_shapes=[pltpu.VMEM((batch, span + SUBLANES, 3 * nh), F32),
                        pltpu.VMEM((batch, nh, nh), F32)],
        compiler_params=pltpu.CompilerParams(
            dimension_semantics=("arbitrary",), vmem_limit_bytes=VMEM_LIMIT),
        name="gated_deltanet",
    )(proj3, proj3, proj3, proj3, proj3, conv_w.astype(F32), rep(a_log), rep(dt_bias),
      jnp.tile(gain.astype(F32), C_HEADS).reshape(1, nh), tri2, bd256)


def _outproj_ffn_kernel(a_ref, b_ref, c_ref, x_ref, gtm_ref, g_ref, sc_ref, sh_ref, gtf_ref,
                        wo_ref, wg_ref, wu_ref, wd_ref, o_ref):
    wa = A_Q_HEADS * HEAD_DIM
    wb = wa + B_HEADS * HEAD_DIM
    y = (_dot(a_ref[...], wo_ref[0:wa, :]) + _dot(b_ref[...], wo_ref[wa:wb, :])
         + _dot(c_ref[...], wo_ref[wb:D_MIX, :]))
    x1 = x_ref[...] + gtm_ref[...] * y
    h = _modulated_norm(x1, g_ref[...], sc_ref[...], sh_ref[...]).astype(BF16)
    act = (_silu(_dot(h, wg_ref[...])) * _dot(h, wu_ref[...])).astype(BF16)
    o_ref[...] = x1 + gtf_ref[...] * _dot(act, wd_ref[...])


def _outproj_ffn(out_a, out_b, out_c, x2, mod5, gain, w_out, w_gate, w_up, w_down, layer, seq):
    n = x2.shape[0]
    tm = min(512, seq)
    tpb = seq // tm
    tile = lambda width: pl.BlockSpec((tm, width), lambda i: (i, 0))
    resident = lambda rows, cols: pl.BlockSpec((None, rows, cols), lambda i: (layer, 0, 0),
                                               pipeline_mode=pl.Buffered(1))
    return pl.pallas_call(
        _outproj_ffn_kernel,
        grid=(n // tm,),
        in_specs=[tile(out_a.shape[1]), tile(out_b.shape[1]), tile(out_c.shape[1]), tile(D_MODEL),
                  _mod_spec(layer, 2, tpb), pl.BlockSpec((1, D_MODEL), lambda i: (0, 0)),
                  _mod_spec(layer, 4, tpb), _mod_spec(layer, 3, tpb), _mod_spec(layer, 5, tpb),
                  resident(D_MIX, D_MODEL), resident(D_MODEL, D_FF), resident(D_MODEL, D_FF),
                  resident(D_FF, D_MODEL)],
        out_specs=tile(D_MODEL),
        out_shape=jax.ShapeDtypeStruct((n, D_MODEL), F32),
        compiler_params=pltpu.CompilerParams(
            dimension_semantics=("arbitrary",), vmem_limit_bytes=VMEM_LIMIT),
        name="outproj_ffn",
    )(out_a, out_b, out_c, x2, mod5, gain, mod5, mod5, mod5, w_out, w_gate, w_up, w_down)


def _head_mean_matrix(size):
    i = np.arange(size) // HEAD_DIM
    return jnp.asarray((i[:, None] == i[None, :]).astype(np.float32) / HEAD_DIM, BF16)


def kernel(x, c, positions, ada_w, ada_b, norm_mix, w_in, attn_q_norm, attn_k_norm, attn_sinks,
           hgrn_lb_logits, hgrn_out_norm, gdn_conv_w, gdn_a_log, gdn_dt_bias, gdn_out_norm, w_out,
           norm_ffn, w_gate, w_up, w_down):
    batch, seq, _ = x.shape
    depth = ada_w.shape[0]
    n = batch * seq
    x2 = x.reshape(n, D_MODEL).astype(F32)

    mod = _modulation(c.astype(F32), ada_w.astype(F32), ada_b.astype(F32))
    mod5 = mod.reshape(depth, 6, batch, 1, D_MODEL)
    cos_t, sin_t = _rope_tables(positions)
    cos3 = cos_t.reshape(batch, seq, LANES)
    sin3 = sin_t.reshape(batch, seq, LANES)
    bd128 = _head_mean_matrix(LANES)
    bd256 = _head_mean_matrix(B_HEADS * HEAD_DIM)

    w_pad = jnp.pad(w_in.astype(BF16), ((0, 0), (0, 0), (0, D_IN_PAD - D_IN)))
    w_out, w_gate, w_up, w_down = (w.astype(BF16) for w in (w_out, w_gate, w_up, w_down))

    for l in range(depth):
        proj = _inproj(x2, mod5, norm_mix[l].astype(F32).reshape(1, D_MODEL), w_pad, l, seq)
        proj3 = proj.reshape(batch, seq, D_IN_PAD)
        out_a = _attention(proj3, cos3, sin3, attn_q_norm[l], attn_k_norm[l], attn_sinks[l],
                           bd128).reshape(n, -1)
        out_b = _hgrn(proj3, hgrn_lb_logits, hgrn_out_norm[l], bd256, l).reshape(n, -1)
        out_c = _gdn(proj3, gdn_conv_w[l], gdn_a_log[l], gdn_dt_bias[l], gdn_out_norm[l],
                     bd256).reshape(n, -1)
        x2 = _outproj_ffn(out_a, out_b, out_c, x2, mod5, norm_ffn[l].astype(F32).reshape(1, D_MODEL),
                          w_out, w_gate, w_up, w_down, l, seq)
    return x2.reshape(batch, seq, D_MODEL).astype(x.dtype)
```

```python
import functools
import math

import numpy as np
import jax
import jax.numpy as jnp
from jax import lax
from jax.experimental import pallas as pl
from jax.experimental.pallas import tpu as pltpu

F32 = jnp.float32
BF16 = jnp.bfloat16

D_MODEL = 1024
HEAD_DIM = 64
A_Q_HEADS = 8
A_KV_HEADS = 2
WINDOW = 128
ROPE_DIM = HEAD_DIM // 4
ROPE_THETA = 500000.0
B_HEADS = 4
C_HEADS = 4
CONV_WIDTH = 4
CHUNK = 64
D_FF = 2816
D_MIX = 1024
D_IN = 2824
NORM_EPS = 1e-6
LOG2E = math.log2(math.e)

LANES = 128
SUBLANES = 8
D_IN_PAD = 2944
GATE_COL = 2816
VMEM_LIMIT = 56 * 1024 * 1024

NN = (((1,), (0,)), ((), ()))
NT = (((1,), (1,)), ((), ()))
TN = (((0,), (0,)), ((), ()))
BNN = (((2,), (1,)), ((0,), (0,)))
BNT = (((2,), (2,)), ((0,), (0,)))


def _dot(a, b, dims=NN):
    return lax.dot_general(a, b, dims, preferred_element_type=F32)


def _split(x, n):
    parts = []
    r = x
    for i in range(n):
        p = r.astype(BF16)
        parts.append(p)
        if i + 1 < n:
            r = r - p.astype(F32)
    return parts


def _dot_exact_r(x, m, n=3):
    out = None
    for p in _split(x, n):
        t = _dot(p, m)
        out = t if out is None else out + t
    return out


def _dot3(a, b, dims=NN):
    a1, a2 = _split(a, 2)
    b1, b2 = _split(b, 2)
    return _dot(a1, b1, dims) + _dot(a1, b2, dims) + _dot(a2, b1, dims)


def _silu(x):
    return x * jax.nn.sigmoid(x)


def _lane_lo(shape):
    return (lax.broadcasted_iota(jnp.int32, shape, len(shape) - 1) & (LANES - 1)) < HEAD_DIM


def _blockdiag2(x):
    lo = _lane_lo(x.shape)
    zero = jnp.zeros_like(x)
    return jnp.concatenate([jnp.where(lo, x, zero), jnp.where(lo, zero, x)], axis=x.ndim - 2)


def _blockdiag(x):
    heads = x.shape[-1] // HEAD_DIM
    lane_head = lax.broadcasted_iota(jnp.int32, x.shape, x.ndim - 1) // HEAD_DIM
    zero = jnp.zeros_like(x)
    return jnp.concatenate([jnp.where(lane_head == i, x, zero) for i in range(heads)], axis=x.ndim - 2)


def _blockdiag_t(x):
    heads = x.shape[-1] // HEAD_DIM
    r = x.shape[1]
    xt = jnp.swapaxes(jnp.concatenate([x] * heads, axis=1), 1, 2)
    row_head = lax.broadcasted_iota(jnp.int32, xt.shape, 1) // HEAD_DIM
    col_head = lax.broadcasted_iota(jnp.int32, xt.shape, 2) // r
    return jnp.where(row_head == col_head, xt, jnp.zeros_like(xt))


def _head_rmsnorm(o, bd, gain):
    return o * lax.rsqrt(_dot(jnp.square(o).astype(BF16), bd) + NORM_EPS) * gain


def _log1p_exp_neg_abs(x):
    return jnp.log(1.0 + jnp.exp(-jnp.abs(x)))


def _mod_kernel(c_ref, w_ref, b_ref, o_ref):
    c = c_ref[...]
    o_ref[...] = _dot3(_silu(c), w_ref[...]) + b_ref[...]


def _modulation(c, ada_w, ada_b):
    depth = ada_w.shape[0]
    b = c.shape[0]
    return pl.pallas_call(
        _mod_kernel,
        grid=(depth, 6),
        in_specs=[
            pl.BlockSpec((b, D_MODEL), lambda l, k: (0, 0)),
            pl.BlockSpec((None, D_MODEL, D_MODEL), lambda l, k: (l, 0, k)),
            pl.BlockSpec((None, None, 1, D_MODEL), lambda l, k: (l, k, 0, 0)),
        ],
        out_specs=pl.BlockSpec((None, None, b, D_MODEL), lambda l, k: (l, k, 0, 0)),
        out_shape=jax.ShapeDtypeStruct((depth, 6, b, D_MODEL), F32),
        compiler_params=pltpu.CompilerParams(
            dimension_semantics=("arbitrary", "arbitrary"), vmem_limit_bytes=VMEM_LIMIT),
        name="modulation",
    )(c, ada_w, ada_b.reshape(depth, 6, 1, D_MODEL))


def _rope_kernel(pos_ref, inv_ref, sgn_ref, sel_ref, cos_ref, sin_ref):
    rows = sel_ref.shape[0]
    pos = jnp.broadcast_to(pos_ref[...].astype(F32), (rows, pos_ref.shape[1]))
    posx = None
    for piece in _split(pos, 3):
        t = _dot(piece, sel_ref[...], TN)
        posx = t if posx is None else posx + t
    ang = posx * inv_ref[...]
    cos_ref[...] = jnp.cos(ang)
    sin_ref[...] = jnp.sin(ang) * sgn_ref[...]


def _rope_tables(positions):
    n = positions.size
    tm = min(2048, n)
    half = ROPE_DIM // 2
    inv_freq = ROPE_THETA ** (-jnp.arange(half, dtype=F32) * 2.0 / ROPE_DIM)
    lane = np.arange(LANES) % HEAD_DIM
    inv_lane = jnp.where(lane < ROPE_DIM, inv_freq[lane % half], 0.0).reshape(1, LANES).astype(F32)
    sgn_lane = jnp.asarray(np.where(lane < half, -1.0, np.where(lane < ROPE_DIM, 1.0, 0.0)),
                           F32).reshape(1, LANES)
    sel_rows = 2 * SUBLANES
    sel = jnp.asarray(np.arange(sel_rows)[:, None] == 0, BF16) * jnp.ones((1, LANES), BF16)
    return pl.pallas_call(
        _rope_kernel,
        grid=(n // tm,),
        in_specs=[
            pl.BlockSpec((None, 1, tm), lambda i: (i, 0, 0)),
            pl.BlockSpec((1, LANES), lambda i: (0, 0)),
            pl.BlockSpec((1, LANES), lambda i: (0, 0)),
            pl.BlockSpec((sel_rows, LANES), lambda i: (0, 0)),
        ],
        out_specs=[pl.BlockSpec((tm, LANES), lambda i: (i, 0))] * 2,
        out_shape=[jax.ShapeDtypeStruct((n, LANES), F32)] * 2,
        compiler_params=pltpu.CompilerParams(dimension_semantics=("arbitrary",)),
        name="rope_tables",
    )(positions.reshape(n // tm, 1, tm), inv_lane, sgn_lane, sel)


def _modulated_norm(x, gain, scale, shift):
    ms = jnp.mean(x * x, axis=-1, keepdims=True)
    return x * lax.rsqrt(ms + NORM_EPS) * gain * (1.0 + scale) + shift


def _inproj_kernel(x_ref, g_ref, sc_ref, sh_ref, w_ref, wgate_ref, o_ref):
    hb = _modulated_norm(x_ref[...], g_ref[...], sc_ref[...], sh_ref[...]).astype(BF16)
    o_ref[:, 0:GATE_COL] = _dot(hb, w_ref[...])
    o_ref[:, GATE_COL:D_IN_PAD] = _dot(hb, wgate_ref[...])


def _mod_spec(layer, which, tiles_per_batch):
    return pl.BlockSpec((None, None, None, 1, D_MODEL),
                        lambda i: (layer, which, i // tiles_per_batch, 0, 0))


def _inproj(x2, mod5, gain, w_in, w_gate_cols, layer, seq):
    n = x2.shape[0]
    tm = min(512, seq)
    tpb = seq // tm
    return pl.pallas_call(
        _inproj_kernel,
        grid=(n // tm,),
        in_specs=[
            pl.BlockSpec((tm, D_MODEL), lambda i: (i, 0)),
            pl.BlockSpec((1, D_MODEL), lambda i: (0, 0)),
            _mod_spec(layer, 1, tpb),
            _mod_spec(layer, 0, tpb),
            pl.BlockSpec((None, D_MODEL, GATE_COL), lambda i: (layer, 0, 0), pipeline_mode=pl.Buffered(1)),
            pl.BlockSpec((None, D_MODEL, LANES), lambda i: (layer, 0, 0), pipeline_mode=pl.Buffered(1)),
        ],
        out_specs=pl.BlockSpec((tm, D_IN_PAD), lambda i: (i, 0)),
        out_shape=jax.ShapeDtypeStruct((n, D_IN_PAD), F32),
        compiler_params=pltpu.CompilerParams(
            dimension_semantics=("arbitrary",), vmem_limit_bytes=VMEM_LIMIT),
        name="inproj",
    )(x2, gain, mod5, mod5, w_in, w_gate_cols)


def _attn_consts():
    i = np.arange(LANES)
    within = i % HEAD_DIM
    half = ROPE_DIM // 2
    src = np.where(within < half, i + half, np.where(within < ROPE_DIM, i - half, -1))
    rot = (i[:, None] == src[None, :]).astype(np.float32)
    mean = ((i // HEAD_DIM)[:, None] == (i // HEAD_DIM)[None, :]).astype(np.float32) / HEAD_DIM
    zero = np.zeros((LANES, LANES), np.float32)
    norm_rot = np.block([[mean, zero], [zero, rot]])
    swap = (i[:, None] == ((i + HEAD_DIM) % LANES)[None, :]).astype(np.float32)
    return jnp.asarray(norm_rot, BF16), jnp.asarray(swap, BF16)


def _attn_kernel(sink_ref, q_ref, k_ref, v_ref, cos_ref, sin_ref, qg_ref, kg_ref, nr_ref,
                 swap_ref, o_ref, kvar_ref, vvar_ref):
    n = pl.program_id(0)
    w = WINDOW
    nb = q_ref.shape[0]
    rows = nb * w

    @pl.when(n == 0)
    def _():
        kvar_ref[:, :, 0:w, :] = jnp.zeros((4, nb, w, LANES), BF16)
        vvar_ref[:, :, 0:w, :] = jnp.zeros((4, nb, w, LANES), BF16)

    @pl.when(n > 0)
    def _():
        kvar_ref[:, :, 0:w, :] = kvar_ref[:, :, w:2 * w, :]
        vvar_ref[:, :, 0:w, :] = vvar_ref[:, :, w:2 * w, :]

    cos = cos_ref[...].reshape(rows, LANES)
    sin = sin_ref[...].reshape(rows, LANES)
    norm_rot = nr_ref[...]
    swap = swap_ref[...]
    lo = lax.broadcasted_iota(jnp.int32, (rows, LANES), 1) < HEAD_DIM

    def norm_rope(xp, gain):
        xg = xp * gain
        res = _dot(jnp.concatenate([jnp.square(xp).astype(BF16), xg.astype(BF16)], axis=1), norm_rot)
        r = lax.rsqrt(res[:, 0:LANES] + NORM_EPS)
        return (xg * cos + res[:, LANES:2 * LANES] * sin) * r

    def variants(x):
        xb = x.astype(BF16)
        xs = _dot(xb, swap).astype(BF16)
        zero = jnp.zeros_like(xb)
        return [jnp.where(lo, xb, zero), jnp.where(lo, zero, xs),
                jnp.where(lo, xs, zero), jnp.where(lo, zero, xb)]

    kn = norm_rope(k_ref[...].reshape(rows, LANES), kg_ref[...])
    for i, t in enumerate(variants(kn)):
        kvar_ref[i, :, w:2 * w, :] = t.reshape(nb, w, LANES)
    for i, t in enumerate(variants(v_ref[...].reshape(rows, LANES))):
        vvar_ref[i, :, w:2 * w, :] = t.reshape(nb, w, LANES)

    qi = lax.broadcasted_iota(jnp.int32, (1, w, 2 * w), 1)
    kj = lax.broadcasted_iota(jnp.int32, (1, w, 2 * w), 2)
    delta = qi + w - kj
    valid = (delta >= 0) & (delta < w) & ((n * w + kj - w) >= 0)
    bias = jnp.where(valid, 0.0, -jnp.inf).astype(F32)
    lo3 = lax.broadcasted_iota(jnp.int32, (1, w, LANES), 2) < HEAD_DIM

    for p in range(A_Q_HEADS // 2):
        g = (2 * p) // (A_Q_HEADS // A_KV_HEADS)
        qn = norm_rope(q_ref[:, :, p * LANES:(p + 1) * LANES].reshape(rows, LANES), qg_ref[...])
        qb = (qn * (HEAD_DIM ** -0.5 * LOG2E)).astype(BF16).reshape(nb, w, LANES)
        num, ms, sums = None, [], []
        for par in range(2):
            sink = sink_ref[2 * p + par] * LOG2E
            s = _dot(qb, kvar_ref[2 * g + par], BNT) + bias
            m = jnp.maximum(jnp.max(s, axis=-1, keepdims=True), sink)
            e = jnp.exp2(s - m)
            ms.append(m)
            sums.append(jnp.sum(e, axis=-1, keepdims=True))
            pv = _dot(e.astype(BF16), vvar_ref[2 * g + par], BNN)
            num = pv if num is None else num + pv
        sink_pair = jnp.where(lo3, sink_ref[2 * p] * LOG2E, sink_ref[2 * p + 1] * LOG2E)
        den = jnp.where(lo3, sums[0], sums[1]) + jnp.exp2(sink_pair - jnp.where(lo3, ms[0], ms[1]))
        o_ref[:, :, p * LANES:(p + 1) * LANES] = (num / den).astype(o_ref.dtype)


def _attention(proj3, cos_t, sin_t, q_norm, k_norm, sinks):
    batch, seq, _ = proj3.shape
    nblk = seq // WINDOW
    qw = A_Q_HEADS * HEAD_DIM
    qg = jnp.tile(q_norm.astype(F32), 2).reshape(1, LANES)
    kg = jnp.tile(k_norm.astype(F32), 2).reshape(1, LANES)
    const = lambda shape: pl.BlockSpec(shape, lambda i: (0, 0))
    return pl.pallas_call(
        _attn_kernel,
        grid=(nblk,),
        in_specs=[
            pl.BlockSpec(memory_space=pltpu.SMEM),
            pl.BlockSpec((batch, WINDOW, qw), lambda i: (0, i, 0)),
            pl.BlockSpec((batch, WINDOW, LANES), lambda i: (0, i, qw // LANES)),
            pl.BlockSpec((batch, WINDOW, LANES), lambda i: (0, i, qw // LANES + 1)),
            pl.BlockSpec((batch, WINDOW, LANES), lambda i: (0, i, 0)),
            pl.BlockSpec((batch, WINDOW, LANES), lambda i: (0, i, 0)),
            const((1, LANES)), const((1, LANES)), const((2 * LANES, 2 * LANES)), const((LANES, LANES)),
        ],
        out_specs=pl.BlockSpec((batch, WINDOW, qw), lambda i: (0, i, 0)),
        out_shape=jax.ShapeDtypeStruct((batch, seq, qw), BF16),
        scratch_shapes=[pltpu.VMEM((4, batch, 2 * WINDOW, LANES), BF16),
                        pltpu.VMEM((4, batch, 2 * WINDOW, LANES), BF16)],
        compiler_params=pltpu.CompilerParams(
            dimension_semantics=("arbitrary",), vmem_limit_bytes=VMEM_LIMIT),
        name="attention",
    )(sinks.astype(F32), proj3, proj3, proj3, cos_t, sin_t, qg, kg, *_attn_consts())


def _hgrn_consts():
    c = CHUNK
    t = np.arange(c)[:, None]
    r = np.arange(c)[None, :]
    mall = np.concatenate([(r <= t), (r > t)], axis=0).astype(np.float32)
    s = np.arange(LANES)[None, :] % c
    x = t ^ s
    lvl = np.where(t > s, np.floor(np.log2(np.maximum(x, 1))).astype(np.int32),
                   np.where(t == s, -1, -2)).astype(np.int32)
    return jnp.asarray(mall, BF16), jnp.asarray(lvl, jnp.int32)


def _head_block_mask(nh):
    ri = lax.broadcasted_iota(jnp.int32, (nh, nh), 0) // HEAD_DIM
    ci = lax.broadcasted_iota(jnp.int32, (nh, nh), 1) // HEAD_DIM
    return ri == ci


def _hgrn_kernel(q_ref, z_ref, v_ref, gt_ref, lbl_ref, gain_ref, mall_ref, lvl_ref, bd_ref,
                 o_ref, st_ref, *, layer):
    c = CHUNK
    nh = B_HEADS * HEAD_DIM
    batch, span, _ = q_ref.shape
    nb = batch
    rows = nb * c

    @pl.when(pl.program_id(0) == 0)
    def _():
        st_ref[...] = jnp.zeros_like(st_ref)

    q = q_ref[...].reshape(rows, nh)
    z = z_ref[...].reshape(rows, nh)
    v = v_ref[...].reshape(rows, nh)

    log_sig = jnp.minimum(z, 0.0) - _log1p_exp_neg_abs(z)
    if layer == 0:
        lf = log_sig
        kk = jax.nn.sigmoid(-z)
    else:
        lg = lbl_ref[...]
        e = jnp.exp(lg - jnp.max(lg, axis=0, keepdims=True))
        sm = e / jnp.sum(e, axis=0, keepdims=True)
        lb = sm[1:2, :]
        for j in range(2, layer + 1):
            lb = lb + sm[j:j + 1, :]
        a = jnp.log(lb)
        b = jnp.log1p(-lb) + log_sig
        lf = jnp.maximum(a, b) + _log1p_exp_neg_abs(a - b)
        kk = (1.0 - lb) * jax.nn.sigmoid(-z)

    lf3 = lf.reshape(nb, c, nh)
    pieces = _split(lf3, 3)
    m_cr = jnp.broadcast_to(mall_ref[0:2 * c, :][None], (nb, 2 * c, c))
    cr = _dot(m_cr, pieces[0], BNN) + _dot(m_cr, pieces[1], BNN) + _dot(m_cr, pieces[2], BNN)
    cum = cr[:, 0:c]
    rest = cr[:, c:2 * c]

    trow_full = lax.broadcasted_iota(jnp.int32, (1, c, nh), 1)
    cum8 = cum.reshape(nb, c // SUBLANES, SUBLANES, nh)
    sub = lax.broadcasted_iota(jnp.int32, (1, 1, SUBLANES, nh), 2)
    dlv = [jnp.where((trow_full & 1) == 1, lf3, jnp.zeros_like(lf3))]
    for l in range(1, 6):
        m = 1 << l
        if 2 * m > SUBLANES:
            ref = jnp.concatenate(
                [jnp.broadcast_to(cum[:, j + m - 1:j + m, :], (nb, 2 * m, nh)) for j in range(0, c, 2 * m)],
                axis=1)
        elif 2 * m == SUBLANES:
            ref = jnp.broadcast_to(cum8[:, :, m - 1:m, :], cum8.shape).reshape(nb, c, nh)
        else:
            ref = jnp.where(sub < 2 * m, jnp.broadcast_to(cum8[:, :, m - 1:m, :], cum8.shape),
                            jnp.broadcast_to(cum8[:, :, 3 * m - 1:3 * m, :], cum8.shape)).reshape(nb, c, nh)
        diff = cum - ref
        dlv.append(jnp.where(((trow_full >> l) & 1) == 1, diff, -diff))

    q3 = q.reshape(nb, c, nh)
    k3 = kk.reshape(nb, c, nh)
    vb = v.reshape(nb, c, nh).astype(BF16)
    lvl = lvl_ref[...][None]
    trow = lax.broadcasted_iota(jnp.int32, (1, c, LANES), 1)
    o_intra = []
    for p in range(B_HEADS // 2):
        sl = slice(p * LANES, (p + 1) * LANES)
        qp, kp = q3[:, :, sl], k3[:, :, sl]
        pm = jnp.where(lvl == -1, _dot(qp.astype(BF16), _blockdiag_t(kp.astype(BF16)), BNN),
                       jnp.zeros((nb, c, LANES), F32))
        for l in range(6):
            is_q = ((trow >> l) & 1) == 1
            xl = (jnp.where(is_q, qp, kp) * jnp.exp(dlv[l][:, :, sl])).astype(BF16)
            pm = jnp.where(lvl == l, _dot(xl, _blockdiag_t(xl), BNN), pm)
        o_intra.append(_dot(pm.astype(BF16), _blockdiag(vb[:, :, sl]), BNN))
    o_intra = jnp.concatenate(o_intra, axis=2)

    q_dec = (q3 * jnp.exp(cum)).astype(BF16)
    k_dec = (k3 * jnp.exp(rest)).astype(BF16)
    dec_last = jnp.exp(cum[:, c - 1:c, :])
    blk = _head_block_mask(nh)[None]
    st = st_ref[...]
    st_t = jnp.swapaxes(st.astype(BF16), 1, 2)
    o_inter = _dot(q_dec, jnp.where(blk, st_t, jnp.zeros_like(st_t)), BNN)
    upd = _dot(jnp.swapaxes(vb, 1, 2), k_dec, BNN)
    st_ref[...] = st * dec_last + jnp.where(blk, upd, jnp.zeros_like(upd))
    o = (o_intra + o_inter).reshape(rows, nh)

    y = _head_rmsnorm(o, bd_ref[...], gain_ref[...])
    o_ref[...] = (y * _silu(gt_ref[...].reshape(rows, nh))).reshape(batch, span, nh).astype(o_ref.dtype)


def _hgrn(proj3, lb_logits, gain, bd256, layer):
    batch, seq, _ = proj3.shape
    span = CHUNK
    nh = B_HEADS * HEAD_DIM
    depth = lb_logits.shape[0]
    mall, lvl = _hgrn_consts()
    col0 = (A_Q_HEADS + 2 * A_KV_HEADS) * HEAD_DIM // nh
    const = lambda shape: pl.BlockSpec(shape, lambda i: (0, 0))
    return pl.pallas_call(
        functools.partial(_hgrn_kernel, layer=layer),
        grid=(seq // span,),
        in_specs=[pl.BlockSpec((batch, span, nh), lambda i, k=k: (0, i, col0 + k)) for k in range(4)]
        + [const((depth, nh)), const((1, nh)), const(mall.shape), const(lvl.shape), const((nh, nh))],
        out_specs=pl.BlockSpec((batch, span, nh), lambda i: (0, i, 0)),
        out_shape=jax.ShapeDtypeStruct((batch, seq, nh), BF16),
        scratch_shapes=[pltpu.VMEM((batch, nh, nh), F32)],
        compiler_params=pltpu.CompilerParams(
            dimension_semantics=("arbitrary",), vmem_limit_bytes=VMEM_LIMIT),
        name="hgrn2",
    )(proj3, proj3, proj3, proj3, lb_logits.astype(F32),
      jnp.tile(gain.astype(F32), B_HEADS).reshape(1, nh), mall, lvl, bd256)


def _gdn_consts():
    c = CHUNK
    t = np.arange(c)[:, None]
    r = np.arange(c)[None, :]
    tri2 = np.concatenate([(r <= t), (r > t)], axis=0).astype(np.float32)
    return jnp.asarray(tri2, BF16)


def _gdn_kernel(cq_ref, ck_ref, cv_ref, cg_ref, gate_ref, cw_ref, alog_ref, dt_ref, gain_ref,
                tri_ref, bd_ref, o_ref, xbuf_ref, s_ref):
    c = CHUNK
    nh = C_HEADS * HEAD_DIM
    batch, span, _ = cq_ref.shape
    nb = batch
    rows = nb * c
    tail = SUBLANES

    @pl.when(pl.program_id(0) == 0)
    def _():
        s_ref[...] = jnp.zeros_like(s_ref)
        xbuf_ref[:, 0:tail, :] = jnp.zeros((batch, tail, 3 * nh), F32)

    xbuf_ref[:, tail:tail + span, 0:nh] = cq_ref[...]
    xbuf_ref[:, tail:tail + span, nh:2 * nh] = ck_ref[...]
    xbuf_ref[:, tail:tail + span, 2 * nh:3 * nh] = cv_ref[...]
    w = cw_ref[...]
    groups = span // tail
    x9 = xbuf_ref[...].reshape(batch, groups + 1, tail, 3 * nh)
    sub = lax.broadcasted_iota(jnp.int32, (1, 1, tail, 3 * nh), 2)
    y = x9[:, 1:] * w[CONV_WIDTH - 1:CONV_WIDTH, :][None, None]
    for k in range(1, CONV_WIDTH):
        rk = pltpu.roll(x9, k, 2)
        shifted = jnp.where(sub >= k, rk[:, 1:], rk[:, :groups])
        y = y + shifted * w[CONV_WIDTH - 1 - k:CONV_WIDTH - k, :][None, None]
    xbuf_ref[:, 0:tail, :] = xbuf_ref[:, span:span + tail, :]
    y = _silu(y).reshape(rows, 3 * nh)
    q, k, v = y[:, 0:nh], y[:, nh:2 * nh], y[:, 2 * nh:3 * nh]

    bd = bd_ref[...]
    q = q * (lax.rsqrt(_dot(jnp.square(q).astype(BF16), bd) + NORM_EPS / HEAD_DIM) * (1.0 / HEAD_DIM))
    k = k * (lax.rsqrt(_dot(jnp.square(k).astype(BF16), bd) + NORM_EPS / HEAD_DIM) * (HEAD_DIM ** -0.5))

    gb = gate_ref[...].reshape(rows, LANES)
    head = lax.broadcasted_iota(jnp.int32, (rows, nh), 1) // HEAD_DIM

    def per_head(col0):
        out = jnp.broadcast_to(gb[:, col0:col0 + 1], (rows, nh))
        for h in range(1, C_HEADS):
            out = jnp.where(head == h, jnp.broadcast_to(gb[:, col0 + h:col0 + h + 1], (rows, nh)), out)
        return out

    beta = jax.nn.sigmoid(per_head(0))
    xg = per_head(C_HEADS) + dt_ref[...]
    g = -jnp.exp(alog_ref[...]) * (jnp.maximum(xg, 0.0) + _log1p_exp_neg_abs(xg))

    g3 = g.reshape(nb, c, nh)
    tri2 = jnp.broadcast_to(tri_ref[...][None], (nb, 2 * c, c))
    g_pieces = _split(g3, 3)
    gsum = None
    for piece in g_pieces:
        t = _dot(tri2, piece, BNN)
        gsum = t if gsum is None else gsum + t
    gcum = gsum[:, 0:c]
    grest = gsum[:, c:2 * c]
    exp_g = jnp.exp(gcum)
    q3 = q.reshape(nb, c, nh)
    k3 = k.reshape(nb, c, nh)
    beta3 = beta.reshape(nb, c, nh)
    kb = k3 * beta3
    vb = v.reshape(nb, c, nh) * beta3
    kbg = kb * exp_g

    ti = lax.broadcasted_iota(jnp.int32, (1, c, nh), 1)
    si = lax.broadcasted_iota(jnp.int32, (1, c, nh), 2) & (HEAD_DIM - 1)
    incl = ti >= si
    strict = ti > si
    zero = jnp.zeros((nb, c, nh), F32)
    eye = jnp.where(ti == si, 1.0, 0.0).astype(F32)
    same16 = (ti >> 4) == (si >> 4)

    def pmm(a, b):
        return _dot(a.astype(BF16), _blockdiag(b.astype(BF16)), BNN)

    def pmm2(a, y):
        yb = y.astype(BF16)
        rhs = jnp.concatenate([_blockdiag(yb[:, :, 0:nh]), _blockdiag(yb[:, :, nh:2 * nh])], axis=2)
        return _dot(a.astype(BF16), rhs, BNN)

    ldiff = None
    for piece in g_pieces[0:2]:
        t = _dot(tri2[:, 0:c], jnp.where(strict, piece, jnp.zeros_like(piece)), BNN)
        ldiff = t if ldiff is None else ldiff + t
    lmat = jnp.where(incl, jnp.exp(jnp.where(incl, ldiff, zero)), zero)
    kbd_t = _blockdiag_t(k3.astype(BF16))
    amat = jnp.where(strict, _dot(kb.astype(BF16), kbd_t, BNN) * lmat, zero)
    dmat = jnp.where(same16, amat, zero)
    noff = amat - dmat
    bm = -dmat
    b2 = pmm(bm, bm)
    b4 = pmm(b2, b2)
    b8 = pmm(b4, b4)
    td = eye + bm
    td = td + pmm(td, b2)
    td = td + pmm(td, b4)
    td = td + pmm(td, b8)
    mm = pmm(td, noff)
    y0 = pmm2(td, jnp.concatenate([vb, kbg], axis=2))
    m2 = pmm(mm, mm)
    y1 = y0 + pmm2(m2, y0)
    y2 = y1 - pmm2(mm, y1)
    u = y2[:, :, 0:nh]
    wmat = y2[:, :, nh:2 * nh].astype(BF16)
    qk = jnp.where(incl, _dot(q3.astype(BF16), kbd_t, BNN) * lmat, zero).astype(BF16)

    q_dec = (q3 * exp_g).astype(BF16)
    k_dec = (k3 * jnp.exp(grest)).astype(BF16)
    dec_last = jnp.exp(gcum[:, c - 1:c, :])
    blk = _head_block_mask(nh)[None]
    s = s_ref[...]
    sb = s.astype(BF16)
    v_new = (u - _dot(wmat, sb, BNN)).astype(BF16)
    o = _dot(q_dec, sb, BNN) + _dot(qk, _blockdiag(v_new), BNN)
    upd = _dot(jnp.swapaxes(k_dec, 1, 2), v_new, BNN)
    s_ref[...] = s * dec_last + jnp.where(blk, upd, jnp.zeros_like(upd))
    o = o.reshape(rows, nh)

    yo = _head_rmsnorm(o, bd, gain_ref[...])
    o_ref[...] = (yo * _silu(cg_ref[...].reshape(rows, nh))).reshape(batch, span, nh).astype(o_ref.dtype)


def _gdn(proj3, conv_w, a_log, dt_bias, gain, bd256):
    batch, seq, _ = proj3.shape
    span = CHUNK
    nh = C_HEADS * HEAD_DIM
    tri2 = _gdn_consts()
    col0 = ((A_Q_HEADS + 2 * A_KV_HEADS) * HEAD_DIM + 4 * nh) // nh
    const = lambda shape: pl.BlockSpec(shape, lambda i: (0, 0))
    rep = lambda a: jnp.repeat(a.astype(F32), HEAD_DIM).reshape(1, nh)
    return pl.pallas_call(
        _gdn_kernel,
        grid=(seq // span,),
        in_specs=[pl.BlockSpec((batch, span, nh), lambda i, k=k: (0, i, col0 + k)) for k in range(4)]
        + [pl.BlockSpec((batch, span, LANES), lambda i: (0, i, GATE_COL // LANES)),
           const((CONV_WIDTH, 3 * nh)), const((1, nh)), const((1, nh)), const((1, nh)),
           const(tri2.shape), const((nh, nh))],
        out_specs=pl.BlockSpec((batch, span, nh), lambda i: (0, i, 0)),
        out_shape=jax.ShapeDtypeStruct((batch, seq, nh), BF16),
        scratch_shapes=[pltpu.VMEM((batch, span + SUBLANES, 3 * nh), F32),
                        pltpu.VMEM((batch, nh, nh), F32)],
        compiler_params=pltpu.CompilerParams(
            dimension_semantics=("arbitrary",), vmem_limit_bytes=VMEM_LIMIT),
        name="gated_deltanet",
    )(proj3, proj3, proj3, proj3, proj3, conv_w.astype(F32), rep(a_log), rep(dt_bias),
      jnp.tile(gain.astype(F32), C_HEADS).reshape(1, nh), tri2, bd256)


def _outproj_ffn_kernel(a_ref, b_ref, c_ref, x_ref, gtm_ref, g_ref, sc_ref, sh_ref, gtf_ref,
                        wo_ref, wg_ref, wu_ref, wd_ref, o_ref):
    wa = A_Q_HEADS * HEAD_DIM
    wb = wa + B_HEADS * HEAD_DIM
    y = (_dot(a_ref[...], wo_ref[0:wa, :]) + _dot(b_ref[...], wo_ref[wa:wb, :])
         + _dot(c_ref[...], wo_ref[wb:D_MIX, :]))
    x1 = x_ref[...] + gtm_ref[...] * y
    h = _modulated_norm(x1, g_ref[...], sc_ref[...], sh_ref[...]).astype(BF16)
    act = (_silu(_dot(h, wg_ref[...])) * _dot(h, wu_ref[...])).astype(BF16)
    o_ref[...] = x1 + gtf_ref[...] * _dot(act, wd_ref[...])


def _outproj_ffn(out_a, out_b, out_c, x2, mod5, gain, w_out, w_gate, w_up, w_down, layer, seq):
    n = x2.shape[0]
    tm = min(512, seq)
    tpb = seq // tm
    tile = lambda width: pl.BlockSpec((tm, width), lambda i: (i, 0))
    resident = lambda rows, cols: pl.BlockSpec((None, rows, cols), lambda i: (layer, 0, 0),
                                               pipeline_mode=pl.Buffered(1))
    return pl.pallas_call(
        _outproj_ffn_kernel,
        grid=(n // tm,),
        in_specs=[tile(out_a.shape[1]), tile(out_b.shape[1]), tile(out_c.shape[1]), tile(D_MODEL),
                  _mod_spec(layer, 2, tpb), pl.BlockSpec((1, D_MODEL), lambda i: (0, 0)),
                  _mod_spec(layer, 4, tpb), _mod_spec(layer, 3, tpb), _mod_spec(layer, 5, tpb),
                  resident(D_MIX, D_MODEL), resident(D_MODEL, D_FF), resident(D_MODEL, D_FF),
                  resident(D_FF, D_MODEL)],
        out_specs=tile(D_MODEL),
        out_shape=jax.ShapeDtypeStruct((n, D_MODEL), F32),
        compiler_params=pltpu.CompilerParams(
            dimension_semantics=("arbitrary",), vmem_limit_bytes=VMEM_LIMIT),
        name="outproj_ffn",
    )(out_a, out_b, out_c, x2, mod5, gain, mod5, mod5, mod5, w_out, w_gate, w_up, w_down)


def _head_mean_matrix(size):
    i = np.arange(size) // HEAD_DIM
    return jnp.asarray((i[:, None] == i[None, :]).astype(np.float32) / HEAD_DIM, BF16)


def kernel(x, c, positions, ada_w, ada_b, norm_mix, w_in, attn_q_norm, attn_k_norm, attn_sinks,
           hgrn_lb_logits, hgrn_out_norm, gdn_conv_w, gdn_a_log, gdn_dt_bias, gdn_out_norm, w_out,
           norm_ffn, w_gate, w_up, w_down):
    batch, seq, _ = x.shape
    depth = ada_w.shape[0]
    n = batch * seq
    x2 = x.reshape(n, D_MODEL).astype(F32)

    mod = _modulation(c.astype(F32), ada_w.astype(F32), ada_b.astype(F32))
    mod5 = mod.reshape(depth, 6, batch, 1, D_MODEL)
    cos_t, sin_t = _rope_tables(positions)
    cos3 = cos_t.reshape(batch, seq, LANES)
    sin3 = sin_t.reshape(batch, seq, LANES)
    bd256 = _head_mean_matrix(B_HEADS * HEAD_DIM)

    w_in_b = w_in.astype(BF16)
    w_gate_cols = jnp.pad(w_in[:, :, GATE_COL:].astype(BF16), ((0, 0), (0, 0), (0, D_IN_PAD - D_IN)))
    w_out, w_gate, w_up, w_down = (w.astype(BF16) for w in (w_out, w_gate, w_up, w_down))

    for l in range(depth):
        proj = _inproj(x2, mod5, norm_mix[l].astype(F32).reshape(1, D_MODEL), w_in_b, w_gate_cols, l, seq)
        proj3 = proj.reshape(batch, seq, D_IN_PAD)
        out_a = _attention(proj3, cos3, sin3, attn_q_norm[l], attn_k_norm[l],
                           attn_sinks[l]).reshape(n, -1)
        out_b = _hgrn(proj3, hgrn_lb_logits, hgrn_out_norm[l], bd256, l).reshape(n, -1)
        out_c = _gdn(proj3, gdn_conv_w[l], gdn_a_log[l], gdn_dt_bias[l], gdn_out_norm[l],
                     bd256).reshape(n, -1)
        x2 = _outproj_ffn(out_a, out_b, out_c, x2, mod5, norm_ffn[l].astype(F32).reshape(1, D_MODEL),
                          w_out, w_gate, w_up, w_down, l, seq)
    return x2.reshape(batch, seq, D_MODEL).astype(x.dtype)
```

```python
import functools
import math

import numpy as np
import jax
import jax.numpy as jnp
from jax import lax
from jax.experimental import pallas as pl
from jax.experimental.pallas import tpu as pltpu

F32 = jnp.float32
BF16 = jnp.bfloat16

D_MODEL = 1024
HEAD_DIM = 64
A_Q_HEADS = 8
A_KV_HEADS = 2
WINDOW = 128
ROPE_DIM = HEAD_DIM // 4
ROPE_THETA = 500000.0
B_HEADS = 4
C_HEADS = 4
CONV_WIDTH = 4
CHUNK = 64
D_FF = 2816
D_MIX = 1024
D_IN = 2824
NORM_EPS = 1e-6
LOG2E = math.log2(math.e)

LANES = 128
SUBLANES = 8
D_IN_PAD = 2944
GATE_COL = 2816
VMEM_LIMIT = 56 * 1024 * 1024

NN = (((1,), (0,)), ((), ()))
NT = (((1,), (1,)), ((), ()))
TN = (((0,), (0,)), ((), ()))
BNN = (((2,), (1,)), ((0,), (0,)))
BNT = (((2,), (2,)), ((0,), (0,)))


def _dot(a, b, dims=NN):
    return lax.dot_general(a, b, dims, preferred_element_type=F32)


def _split(x, n):
    parts = []
    r = x
    for i in range(n):
        p = r.astype(BF16)
        parts.append(p)
        if i + 1 < n:
            r = r - p.astype(F32)
    return parts


def _dot_exact_r(x, m, n=3):
    out = None
    for p in _split(x, n):
        t = _dot(p, m)
        out = t if out is None else out + t
    return out


def _dot3(a, b, dims=NN):
    a1, a2 = _split(a, 2)
    b1, b2 = _split(b, 2)
    return _dot(a1, b1, dims) + _dot(a1, b2, dims) + _dot(a2, b1, dims)


def _silu(x):
    h = 0.5 * x
    return h + h * jnp.tanh(h)


def _lane_lo(shape):
    return (lax.broadcasted_iota(jnp.int32, shape, len(shape) - 1) & (LANES - 1)) < HEAD_DIM


def _blockdiag2(x):
    lo = _lane_lo(x.shape)
    zero = jnp.zeros_like(x)
    return jnp.concatenate([jnp.where(lo, x, zero), jnp.where(lo, zero, x)], axis=x.ndim - 2)


def _blockdiag(x):
    heads = x.shape[-1] // HEAD_DIM
    lane_head = lax.broadcasted_iota(jnp.int32, x.shape, x.ndim - 1) // HEAD_DIM
    zero = jnp.zeros_like(x)
    return jnp.concatenate([jnp.where(lane_head == i, x, zero) for i in range(heads)], axis=x.ndim - 2)


def _blockdiag_t(x):
    heads = x.shape[-1] // HEAD_DIM
    r = x.shape[1]
    xt = jnp.swapaxes(jnp.concatenate([x] * heads, axis=1), 1, 2)
    row_head = lax.broadcasted_iota(jnp.int32, xt.shape, 1) // HEAD_DIM
    col_head = lax.broadcasted_iota(jnp.int32, xt.shape, 2) // r
    return jnp.where(row_head == col_head, xt, jnp.zeros_like(xt))


def _head_rmsnorm(o, bd, gain):
    return o * lax.rsqrt(_dot(jnp.square(o).astype(BF16), bd) + NORM_EPS) * gain


def _log1p_exp_neg_abs(x):
    return jnp.log(1.0 + jnp.exp(-jnp.abs(x)))


def _mod_kernel(c_ref, w_ref, b_ref, o_ref):
    c = c_ref[...]
    o_ref[...] = _dot3(_silu(c), w_ref[...]) + b_ref[...]


def _modulation(c, ada_w, ada_b):
    depth = ada_w.shape[0]
    b = c.shape[0]
    return pl.pallas_call(
        _mod_kernel,
        grid=(depth, 6),
        in_specs=[
            pl.BlockSpec((b, D_MODEL), lambda l, k: (0, 0)),
            pl.BlockSpec((None, D_MODEL, D_MODEL), lambda l, k: (l, 0, k)),
            pl.BlockSpec((None, None, 1, D_MODEL), lambda l, k: (l, k, 0, 0)),
        ],
        out_specs=pl.BlockSpec((None, None, b, D_MODEL), lambda l, k: (l, k, 0, 0)),
        out_shape=jax.ShapeDtypeStruct((depth, 6, b, D_MODEL), F32),
        compiler_params=pltpu.CompilerParams(
            dimension_semantics=("arbitrary", "arbitrary"), vmem_limit_bytes=VMEM_LIMIT),
        name="modulation",
    )(c, ada_w, ada_b.reshape(depth, 6, 1, D_MODEL))


def _rope_kernel(pos_ref, inv_ref, sgn_ref, sel_ref, cos_ref, sin_ref):
    rows = sel_ref.shape[0]
    pos = jnp.broadcast_to(pos_ref[...].astype(F32), (rows, pos_ref.shape[1]))
    posx = None
    for piece in _split(pos, 3):
        t = _dot(piece, sel_ref[...], TN)
        posx = t if posx is None else posx + t
    ang = posx * inv_ref[...]
    cos_ref[...] = jnp.cos(ang)
    sin_ref[...] = jnp.sin(ang) * sgn_ref[...]


def _rope_tables(positions):
    n = positions.size
    tm = min(2048, n)
    half = ROPE_DIM // 2
    inv_freq = ROPE_THETA ** (-jnp.arange(half, dtype=F32) * 2.0 / ROPE_DIM)
    lane = np.arange(LANES) % HEAD_DIM
    inv_lane = jnp.where(lane < ROPE_DIM, inv_freq[lane % half], 0.0).reshape(1, LANES).astype(F32)
    sgn_lane = jnp.asarray(np.where(lane < half, -1.0, np.where(lane < ROPE_DIM, 1.0, 0.0)),
                           F32).reshape(1, LANES)
    sel_rows = 2 * SUBLANES
    sel = jnp.asarray(np.arange(sel_rows)[:, None] == 0, BF16) * jnp.ones((1, LANES), BF16)
    return pl.pallas_call(
        _rope_kernel,
        grid=(n // tm,),
        in_specs=[
            pl.BlockSpec((None, 1, tm), lambda i: (i, 0, 0)),
            pl.BlockSpec((1, LANES), lambda i: (0, 0)),
            pl.BlockSpec((1, LANES), lambda i: (0, 0)),
            pl.BlockSpec((sel_rows, LANES), lambda i: (0, 0)),
        ],
        out_specs=[pl.BlockSpec((tm, LANES), lambda i: (i, 0))] * 2,
        out_shape=[jax.ShapeDtypeStruct((n, LANES), F32)] * 2,
        compiler_params=pltpu.CompilerParams(dimension_semantics=("arbitrary",)),
        name="rope_tables",
    )(positions.reshape(n // tm, 1, tm), inv_lane, sgn_lane, sel)


def _modulated_norm(x, gain, scale, shift):
    ms = jnp.mean(x * x, axis=-1, keepdims=True)
    return x * lax.rsqrt(ms + NORM_EPS) * gain * (1.0 + scale) + shift


def _inproj_kernel(x_ref, g_ref, sc_ref, sh_ref, w_ref, wgate_ref, o_ref):
    hb = _modulated_norm(x_ref[...], g_ref[...], sc_ref[...], sh_ref[...]).astype(BF16)
    o_ref[:, 0:GATE_COL] = _dot(hb, w_ref[...])
    o_ref[:, GATE_COL:D_IN_PAD] = _dot(hb, wgate_ref[...])


def _mod_spec(layer, which, tiles_per_batch):
    return pl.BlockSpec((None, None, None, 1, D_MODEL),
                        lambda i: (layer, which, i // tiles_per_batch, 0, 0))


def _inproj(x2, mod5, gain, w_in, w_gate_cols, layer, seq):
    n = x2.shape[0]
    tm = min(1024, seq)
    tpb = seq // tm
    return pl.pallas_call(
        _inproj_kernel,
        grid=(n // tm,),
        in_specs=[
            pl.BlockSpec((tm, D_MODEL), lambda i: (i, 0)),
            pl.BlockSpec((1, D_MODEL), lambda i: (0, 0)),
            _mod_spec(layer, 1, tpb),
            _mod_spec(layer, 0, tpb),
            pl.BlockSpec((None, D_MODEL, GATE_COL), lambda i: (layer, 0, 0), pipeline_mode=pl.Buffered(1)),
            pl.BlockSpec((None, D_MODEL, LANES), lambda i: (layer, 0, 0), pipeline_mode=pl.Buffered(1)),
        ],
        out_specs=pl.BlockSpec((tm, D_IN_PAD), lambda i: (i, 0)),
        out_shape=jax.ShapeDtypeStruct((n, D_IN_PAD), F32),
        compiler_params=pltpu.CompilerParams(
            dimension_semantics=("arbitrary",), vmem_limit_bytes=VMEM_LIMIT),
        name="inproj",
    )(x2, gain, mod5, mod5, w_in, w_gate_cols)


def _attn_consts():
    i = np.arange(LANES)
    within = i % HEAD_DIM
    half = ROPE_DIM // 2
    src = np.where(within < half, i + half, np.where(within < ROPE_DIM, i - half, -1))
    rot = (i[:, None] == src[None, :]).astype(np.float32)
    mean = ((i // HEAD_DIM)[:, None] == (i // HEAD_DIM)[None, :]).astype(np.float32) / HEAD_DIM
    zero = np.zeros((LANES, LANES), np.float32)
    norm_rot = np.block([[mean, zero], [zero, rot]])
    swap = (i[:, None] == ((i + HEAD_DIM) % LANES)[None, :]).astype(np.float32)
    return jnp.asarray(norm_rot, BF16), jnp.asarray(swap, BF16)


def _attn_kernel(sink_ref, q_ref, k_ref, v_ref, cos_ref, sin_ref, qg_ref, kg_ref, nr_ref,
                 swap_ref, o_ref, kvar_ref, vvar_ref):
    n = pl.program_id(0)
    w = WINDOW
    nb = q_ref.shape[0]
    rows = nb * w

    @pl.when(n == 0)
    def _():
        kvar_ref[:, :, 0:w, :] = jnp.zeros((4, nb, w, LANES), BF16)
        vvar_ref[:, :, 0:w, :] = jnp.zeros((4, nb, w, LANES), BF16)

    @pl.when(n > 0)
    def _():
        kvar_ref[:, :, 0:w, :] = kvar_ref[:, :, w:2 * w, :]
        vvar_ref[:, :, 0:w, :] = vvar_ref[:, :, w:2 * w, :]

    cos = cos_ref[...].reshape(rows, LANES)
    sin = sin_ref[...].reshape(rows, LANES)
    norm_rot = nr_ref[...]
    swap = swap_ref[...]
    lo = lax.broadcasted_iota(jnp.int32, (rows, LANES), 1) < HEAD_DIM

    def norm_rope(xp, gain):
        xg = xp * gain
        res = _dot(jnp.concatenate([jnp.square(xp).astype(BF16), xg.astype(BF16)], axis=1), norm_rot)
        r = lax.rsqrt(res[:, 0:LANES] + NORM_EPS)
        return (xg * cos + res[:, LANES:2 * LANES] * sin) * r

    def variants(x):
        xb = x.astype(BF16)
        xs = _dot(xb, swap).astype(BF16)
        zero = jnp.zeros_like(xb)
        return [jnp.where(lo, xb, zero), jnp.where(lo, zero, xs),
                jnp.where(lo, xs, zero), jnp.where(lo, zero, xb)]

    kn = norm_rope(k_ref[...].reshape(rows, LANES), kg_ref[...])
    for i, t in enumerate(variants(kn)):
        kvar_ref[i, :, w:2 * w, :] = t.reshape(nb, w, LANES)
    for i, t in enumerate(variants(v_ref[...].reshape(rows, LANES))):
        vvar_ref[i, :, w:2 * w, :] = t.reshape(nb, w, LANES)

    qi = lax.broadcasted_iota(jnp.int32, (1, w, 2 * w), 1)
    kj = lax.broadcasted_iota(jnp.int32, (1, w, 2 * w), 2)
    delta = qi + w - kj
    valid = (delta >= 0) & (delta < w) & ((n * w + kj - w) >= 0)
    bias = jnp.where(valid, 0.0, -jnp.inf).astype(F32)
    lo3 = lax.broadcasted_iota(jnp.int32, (1, w, LANES), 2) < HEAD_DIM

    for p in range(A_Q_HEADS // 2):
        g = (2 * p) // (A_Q_HEADS // A_KV_HEADS)
        qn = norm_rope(q_ref[:, :, p * LANES:(p + 1) * LANES].reshape(rows, LANES), qg_ref[...])
        qb = (qn * (HEAD_DIM ** -0.5 * LOG2E)).astype(BF16).reshape(nb, w, LANES)
        num, ms, sums = None, [], []
        for par in range(2):
            sink = sink_ref[2 * p + par] * LOG2E
            s = _dot(qb, kvar_ref[2 * g + par], BNT) + bias
            m = jnp.maximum(jnp.max(s, axis=-1, keepdims=True), sink)
            e = jnp.exp2(s - m)
            ms.append(m)
            sums.append(jnp.sum(e, axis=-1, keepdims=True))
            pv = _dot(e.astype(BF16), vvar_ref[2 * g + par], BNN)
            num = pv if num is None else num + pv
        sink_pair = jnp.where(lo3, sink_ref[2 * p] * LOG2E, sink_ref[2 * p + 1] * LOG2E)
        den = jnp.where(lo3, sums[0], sums[1]) + jnp.exp2(sink_pair - jnp.where(lo3, ms[0], ms[1]))
        o_ref[:, :, p * LANES:(p + 1) * LANES] = (num / den).astype(o_ref.dtype)


def _attention(proj3, cos_t, sin_t, q_norm, k_norm, sinks):
    batch, seq, _ = proj3.shape
    nblk = seq // WINDOW
    qw = A_Q_HEADS * HEAD_DIM
    qg = jnp.tile(q_norm.astype(F32), 2).reshape(1, LANES)
    kg = jnp.tile(k_norm.astype(F32), 2).reshape(1, LANES)
    const = lambda shape: pl.BlockSpec(shape, lambda i: (0, 0))
    return pl.pallas_call(
        _attn_kernel,
        grid=(nblk,),
        in_specs=[
            pl.BlockSpec(memory_space=pltpu.SMEM),
            pl.BlockSpec((batch, WINDOW, qw), lambda i: (0, i, 0)),
            pl.BlockSpec((batch, WINDOW, LANES), lambda i: (0, i, qw // LANES)),
            pl.BlockSpec((batch, WINDOW, LANES), lambda i: (0, i, qw // LANES + 1)),
            pl.BlockSpec((batch, WINDOW, LANES), lambda i: (0, i, 0)),
            pl.BlockSpec((batch, WINDOW, LANES), lambda i: (0, i, 0)),
            const((1, LANES)), const((1, LANES)), const((2 * LANES, 2 * LANES)), const((LANES, LANES)),
        ],
        out_specs=pl.BlockSpec((batch, WINDOW, qw), lambda i: (0, i, 0)),
        out_shape=jax.ShapeDtypeStruct((batch, seq, qw), BF16),
        scratch_shapes=[pltpu.VMEM((4, batch, 2 * WINDOW, LANES), BF16),
                        pltpu.VMEM((4, batch, 2 * WINDOW, LANES), BF16)],
        compiler_params=pltpu.CompilerParams(
            dimension_semantics=("arbitrary",), vmem_limit_bytes=VMEM_LIMIT),
        name="attention",
    )(sinks.astype(F32), proj3, proj3, proj3, cos_t, sin_t, qg, kg, *_attn_consts())


def _hgrn_consts():
    c = CHUNK
    t = np.arange(c)[:, None]
    r = np.arange(c)[None, :]
    mall = np.concatenate([(r <= t), (r > t)], axis=0).astype(np.float32)
    s = np.arange(LANES)[None, :] % c
    x = t ^ s
    lvl = np.where(t > s, np.floor(np.log2(np.maximum(x, 1))).astype(np.int32),
                   np.where(t == s, -1, -2)).astype(np.int32)
    return jnp.asarray(mall, BF16), jnp.asarray(lvl, jnp.int32)


def _head_block_mask(nh):
    ri = lax.broadcasted_iota(jnp.int32, (nh, nh), 0) // HEAD_DIM
    ci = lax.broadcasted_iota(jnp.int32, (nh, nh), 1) // HEAD_DIM
    return ri == ci


def _hgrn_kernel(q_ref, z_ref, v_ref, gt_ref, lbl_ref, gain_ref, mall_ref, lvl_ref, bd_ref,
                 o_ref, st_ref, *, layer):
    c = CHUNK
    nh = B_HEADS * HEAD_DIM
    batch, span, _ = q_ref.shape
    nb = batch
    rows = nb * c

    @pl.when(pl.program_id(0) == 0)
    def _():
        st_ref[...] = jnp.zeros_like(st_ref)

    q = q_ref[...].reshape(rows, nh)
    z = z_ref[...].reshape(rows, nh)
    v = v_ref[...].reshape(rows, nh)

    log_sig = jnp.minimum(z, 0.0) - _log1p_exp_neg_abs(z)
    if layer == 0:
        lf = log_sig
        kk = jax.nn.sigmoid(-z)
    else:
        lg = lbl_ref[...]
        e = jnp.exp(lg - jnp.max(lg, axis=0, keepdims=True))
        sm = e / jnp.sum(e, axis=0, keepdims=True)
        lb = sm[1:2, :]
        for j in range(2, layer + 1):
            lb = lb + sm[j:j + 1, :]
        a = jnp.log(lb)
        b = jnp.log1p(-lb) + log_sig
        lf = jnp.maximum(a, b) + _log1p_exp_neg_abs(a - b)
        kk = (1.0 - lb) * jax.nn.sigmoid(-z)

    lf3 = lf.reshape(nb, c, nh)
    pieces = _split(lf3, 3)
    m_cr = jnp.broadcast_to(mall_ref[0:2 * c, :][None], (nb, 2 * c, c))
    cr = _dot(m_cr, pieces[0], BNN) + _dot(m_cr, pieces[1], BNN) + _dot(m_cr, pieces[2], BNN)
    cum = cr[:, 0:c]
    rest = cr[:, c:2 * c]

    trow_full = lax.broadcasted_iota(jnp.int32, (1, c, nh), 1)
    cum8 = cum.reshape(nb, c // SUBLANES, SUBLANES, nh)
    sub = lax.broadcasted_iota(jnp.int32, (1, 1, SUBLANES, nh), 2)
    dlv = [jnp.where((trow_full & 1) == 1, lf3, jnp.zeros_like(lf3))]
    for l in range(1, 6):
        m = 1 << l
        if 2 * m > SUBLANES:
            ref = jnp.concatenate(
                [jnp.broadcast_to(cum[:, j + m - 1:j + m, :], (nb, 2 * m, nh)) for j in range(0, c, 2 * m)],
                axis=1)
        elif 2 * m == SUBLANES:
            ref = jnp.broadcast_to(cum8[:, :, m - 1:m, :], cum8.shape).reshape(nb, c, nh)
        else:
            ref = jnp.where(sub < 2 * m, jnp.broadcast_to(cum8[:, :, m - 1:m, :], cum8.shape),
                            jnp.broadcast_to(cum8[:, :, 3 * m - 1:3 * m, :], cum8.shape)).reshape(nb, c, nh)
        diff = cum - ref
        dlv.append(jnp.where(((trow_full >> l) & 1) == 1, diff, -diff))

    q3 = q.reshape(nb, c, nh)
    k3 = kk.reshape(nb, c, nh)
    vb = v.reshape(nb, c, nh).astype(BF16)
    lvl = lvl_ref[...][None]
    trow = lax.broadcasted_iota(jnp.int32, (1, c, LANES), 1)
    o_intra = []
    for p in range(B_HEADS // 2):
        sl = slice(p * LANES, (p + 1) * LANES)
        qp, kp = q3[:, :, sl], k3[:, :, sl]
        pm = jnp.where(lvl == -1, _dot(qp.astype(BF16), _blockdiag_t(kp.astype(BF16)), BNN),
                       jnp.zeros((nb, c, LANES), F32))
        for l in range(6):
            is_q = ((trow >> l) & 1) == 1
            xl = (jnp.where(is_q, qp, kp) * jnp.exp(dlv[l][:, :, sl])).astype(BF16)
            pm = jnp.where(lvl == l, _dot(xl, _blockdiag_t(xl), BNN), pm)
        o_intra.append(_dot(pm.astype(BF16), _blockdiag(vb[:, :, sl]), BNN))
    o_intra = jnp.concatenate(o_intra, axis=2)

    q_dec = (q3 * jnp.exp(cum)).astype(BF16)
    k_dec = (k3 * jnp.exp(rest)).astype(BF16)
    dec_last = jnp.exp(cum[:, c - 1:c, :])
    blk = _head_block_mask(nh)[None]
    st = st_ref[...]
    st_t = jnp.swapaxes(st.astype(BF16), 1, 2)
    o_inter = _dot(q_dec, jnp.where(blk, st_t, jnp.zeros_like(st_t)), BNN)
    upd = _dot(jnp.swapaxes(vb, 1, 2), k_dec, BNN)
    st_ref[...] = st * dec_last + jnp.where(blk, upd, jnp.zeros_like(upd))
    o = (o_intra + o_inter).reshape(rows, nh)

    y = _head_rmsnorm(o, bd_ref[...], gain_ref[...])
    o_ref[...] = (y * _silu(gt_ref[...].reshape(rows, nh))).reshape(batch, span, nh).astype(o_ref.dtype)


def _hgrn_operands(proj3, lb_logits, gain, bd256):
    batch = proj3.shape[0]
    nh = B_HEADS * HEAD_DIM
    depth = lb_logits.shape[0]
    mall, lvl = _hgrn_consts()
    col0 = (A_Q_HEADS + 2 * A_KV_HEADS) * HEAD_DIM // nh
    const = lambda shape: pl.BlockSpec(shape, lambda i: (0, 0))
    in_specs = ([pl.BlockSpec((batch, CHUNK, nh), lambda i, k=k: (0, i, col0 + k)) for k in range(4)]
                + [const((depth, nh)), const((1, nh)), const(mall.shape), const(lvl.shape), const((nh, nh))])
    operands = (proj3, proj3, proj3, proj3, lb_logits.astype(F32),
                jnp.tile(gain.astype(F32), B_HEADS).reshape(1, nh), mall, lvl, bd256)
    return in_specs, operands, [pltpu.VMEM((batch, nh, nh), F32)]


def _gdn_consts():
    c = CHUNK
    t = np.arange(c)[:, None]
    r = np.arange(c)[None, :]
    tri2 = np.concatenate([(r <= t), (r > t)], axis=0).astype(np.float32)
    return jnp.asarray(tri2, BF16)


def _gdn_kernel(cq_ref, ck_ref, cv_ref, cg_ref, gate_ref, cw_ref, alog_ref, dt_ref, gain_ref,
                tri_ref, bd_ref, o_ref, xbuf_ref, s_ref):
    c = CHUNK
    nh = C_HEADS * HEAD_DIM
    batch, span, _ = cq_ref.shape
    nb = batch
    rows = nb * c
    tail = SUBLANES

    @pl.when(pl.program_id(0) == 0)
    def _():
        s_ref[...] = jnp.zeros_like(s_ref)
        xbuf_ref[:, 0:tail, :] = jnp.zeros((batch, tail, 3 * nh), F32)

    xbuf_ref[:, tail:tail + span, 0:nh] = cq_ref[...]
    xbuf_ref[:, tail:tail + span, nh:2 * nh] = ck_ref[...]
    xbuf_ref[:, tail:tail + span, 2 * nh:3 * nh] = cv_ref[...]
    w = cw_ref[...]
    groups = span // tail
    x9 = xbuf_ref[...].reshape(batch, groups + 1, tail, 3 * nh)
    sub = lax.broadcasted_iota(jnp.int32, (1, 1, tail, 3 * nh), 2)
    y = x9[:, 1:] * w[CONV_WIDTH - 1:CONV_WIDTH, :][None, None]
    for k in range(1, CONV_WIDTH):
        rk = pltpu.roll(x9, k, 2)
        shifted = jnp.where(sub >= k, rk[:, 1:], rk[:, :groups])
        y = y + shifted * w[CONV_WIDTH - 1 - k:CONV_WIDTH - k, :][None, None]
    xbuf_ref[:, 0:tail, :] = xbuf_ref[:, span:span + tail, :]
    y = _silu(y).reshape(rows, 3 * nh)
    q, k, v = y[:, 0:nh], y[:, nh:2 * nh], y[:, 2 * nh:3 * nh]

    bd = bd_ref[...]
    q = q * (lax.rsqrt(_dot(jnp.square(q).astype(BF16), bd) + NORM_EPS / HEAD_DIM) * (1.0 / HEAD_DIM))
    k = k * (lax.rsqrt(_dot(jnp.square(k).astype(BF16), bd) + NORM_EPS / HEAD_DIM) * (HEAD_DIM ** -0.5))

    gb = gate_ref[...].reshape(rows, LANES)
    head = lax.broadcasted_iota(jnp.int32, (rows, nh), 1) // HEAD_DIM

    def per_head(col0):
        out = jnp.broadcast_to(gb[:, col0:col0 + 1], (rows, nh))
        for h in range(1, C_HEADS):
            out = jnp.where(head == h, jnp.broadcast_to(gb[:, col0 + h:col0 + h + 1], (rows, nh)), out)
        return out

    beta = jax.nn.sigmoid(per_head(0))
    xg = per_head(C_HEADS) + dt_ref[...]
    g = -jnp.exp(alog_ref[...]) * (jnp.maximum(xg, 0.0) + _log1p_exp_neg_abs(xg))

    g3 = g.reshape(nb, c, nh)
    tri2 = jnp.broadcast_to(tri_ref[...][None], (nb, 2 * c, c))
    g_pieces = _split(g3, 3)
    gsum = None
    for piece in g_pieces:
        t = _dot(tri2, piece, BNN)
        gsum = t if gsum is None else gsum + t
    gcum = gsum[:, 0:c]
    grest = gsum[:, c:2 * c]
    exp_g = jnp.exp(gcum)
    q3 = q.reshape(nb, c, nh)
    k3 = k.reshape(nb, c, nh)
    beta3 = beta.reshape(nb, c, nh)
    kb = k3 * beta3
    vb = v.reshape(nb, c, nh) * beta3
    kbg = kb * exp_g

    ti = lax.broadcasted_iota(jnp.int32, (1, c, nh), 1)
    si = lax.broadcasted_iota(jnp.int32, (1, c, nh), 2) & (HEAD_DIM - 1)
    incl = ti >= si
    strict = ti > si
    zero = jnp.zeros((nb, c, nh), F32)
    eye = jnp.where(ti == si, 1.0, 0.0).astype(F32)
    same16 = (ti >> 4) == (si >> 4)

    def pmm(a, b):
        return _dot(a.astype(BF16), _blockdiag(b.astype(BF16)), BNN)

    def pmm2(a, y):
        yb = y.astype(BF16)
        rhs = jnp.concatenate([_blockdiag(yb[:, :, 0:nh]), _blockdiag(yb[:, :, nh:2 * nh])], axis=2)
        return _dot(a.astype(BF16), rhs, BNN)

    ldiff = None
    for piece in g_pieces[0:2]:
        t = _dot(tri2[:, 0:c], jnp.where(strict, piece, jnp.zeros_like(piece)), BNN)
        ldiff = t if ldiff is None else ldiff + t
    lmat = jnp.where(incl, jnp.exp(jnp.where(incl, ldiff, zero)), zero)
    kbd_t = _blockdiag_t(k3.astype(BF16))
    amat = jnp.where(strict, _dot(kb.astype(BF16), kbd_t, BNN) * lmat, zero)
    dmat = jnp.where(same16, amat, zero)
    noff = amat - dmat
    bm = -dmat
    b2 = pmm(bm, bm)
    b4 = pmm(b2, b2)
    b8 = pmm(b4, b4)
    td = eye + bm
    td = td + pmm(td, b2)
    td = td + pmm(td, b4)
    td = td + pmm(td, b8)
    mm = pmm(td, noff)
    y0 = pmm2(td, jnp.concatenate([vb, kbg], axis=2))
    m2 = pmm(mm, mm)
    y1 = y0 + pmm2(m2, y0)
    y2 = y1 - pmm2(mm, y1)
    u = y2[:, :, 0:nh]
    wmat = y2[:, :, nh:2 * nh].astype(BF16)
    qk = jnp.where(incl, _dot(q3.astype(BF16), kbd_t, BNN) * lmat, zero).astype(BF16)

    q_dec = (q3 * exp_g).astype(BF16)
    k_dec = (k3 * jnp.exp(grest)).astype(BF16)
    dec_last = jnp.exp(gcum[:, c - 1:c, :])
    blk = _head_block_mask(nh)[None]
    s = s_ref[...]
    sb = s.astype(BF16)
    v_new = (u - _dot(wmat, sb, BNN)).astype(BF16)
    o = _dot(q_dec, sb, BNN) + _dot(qk, _blockdiag(v_new), BNN)
    upd = _dot(jnp.swapaxes(k_dec, 1, 2), v_new, BNN)
    s_ref[...] = s * dec_last + jnp.where(blk, upd, jnp.zeros_like(upd))
    o = o.reshape(rows, nh)

    yo = _head_rmsnorm(o, bd, gain_ref[...])
    o_ref[...] = (yo * _silu(cg_ref[...].reshape(rows, nh))).reshape(batch, span, nh).astype(o_ref.dtype)


def _gdn_operands(proj3, conv_w, a_log, dt_bias, gain, bd256):
    batch = proj3.shape[0]
    nh = C_HEADS * HEAD_DIM
    tri2 = _gdn_consts()
    col0 = ((A_Q_HEADS + 2 * A_KV_HEADS) * HEAD_DIM + 4 * nh) // nh
    const = lambda shape: pl.BlockSpec(shape, lambda i: (0, 0))
    rep = lambda a: jnp.repeat(a.astype(F32), HEAD_DIM).reshape(1, nh)
    in_specs = ([pl.BlockSpec((batch, CHUNK, nh), lambda i, k=k: (0, i, col0 + k)) for k in range(4)]
                + [pl.BlockSpec((batch, CHUNK, LANES), lambda i: (0, i, GATE_COL // LANES)),
                   const((CONV_WIDTH, 3 * nh)), const((1, nh)), const((1, nh)), const((1, nh)),
                   const(tri2.shape), const((nh, nh))])
    operands = (proj3, proj3, proj3, proj3, proj3, conv_w.astype(F32), rep(a_log), rep(dt_bias),
                jnp.tile(gain.astype(F32), C_HEADS).reshape(1, nh), tri2, bd256)
    scratch = [pltpu.VMEM((batch, CHUNK + SUBLANES, 3 * nh), F32), pltpu.VMEM((batch, nh, nh), F32)]
    return in_specs, operands, scratch


def _recurrent_kernel(*refs, layer, n_hgrn_in, n_gdn_in):
    h_in = refs[0:n_hgrn_in]
    g_in = refs[n_hgrn_in:n_hgrn_in + n_gdn_in]
    o_h, o_g, st_h, xbuf, s_g = refs[n_hgrn_in + n_gdn_in:]
    _hgrn_kernel(*h_in, o_h, st_h, layer=layer)
    _gdn_kernel(*g_in, o_g, xbuf, s_g)


def _recurrent_mixers(proj3, lb_logits, hgrn_gain, conv_w, a_log, dt_bias, gdn_gain, bd256, layer):
    batch, seq, _ = proj3.shape
    nh = B_HEADS * HEAD_DIM
    h_specs, h_ops, h_scratch = _hgrn_operands(proj3, lb_logits, hgrn_gain, bd256)
    g_specs, g_ops, g_scratch = _gdn_operands(proj3, conv_w, a_log, dt_bias, gdn_gain, bd256)
    out_spec = pl.BlockSpec((batch, CHUNK, nh), lambda i: (0, i, 0))
    return pl.pallas_call(
        functools.partial(_recurrent_kernel, layer=layer, n_hgrn_in=len(h_ops), n_gdn_in=len(g_ops)),
        grid=(seq // CHUNK,),
        in_specs=h_specs + g_specs,
        out_specs=[out_spec, out_spec],
        out_shape=[jax.ShapeDtypeStruct((batch, seq, nh), BF16)] * 2,
        scratch_shapes=h_scratch + g_scratch,
        compiler_params=pltpu.CompilerParams(
            dimension_semantics=("arbitrary",), vmem_limit_bytes=VMEM_LIMIT),
        name="recurrent_mixers",
    )(*h_ops, *g_ops)


def _outproj_ffn_kernel(a_ref, b_ref, c_ref, x_ref, gtm_ref, g_ref, sc_ref, sh_ref, gtf_ref,
                        wo_ref, wg_ref, wu_ref, wd_ref, o_ref):
    wa = A_Q_HEADS * HEAD_DIM
    wb = wa + B_HEADS * HEAD_DIM
    y = (_dot(a_ref[...], wo_ref[0:wa, :]) + _dot(b_ref[...], wo_ref[wa:wb, :])
         + _dot(c_ref[...], wo_ref[wb:D_MIX, :]))
    x1 = x_ref[...] + gtm_ref[...] * y
    h = _modulated_norm(x1, g_ref[...], sc_ref[...], sh_ref[...]).astype(BF16)
    act = (_silu(_dot(h, wg_ref[...])) * _dot(h, wu_ref[...])).astype(BF16)
    o_ref[...] = x1 + gtf_ref[...] * _dot(act, wd_ref[...])


def _outproj_ffn(out_a, out_b, out_c, x2, mod5, gain, w_out, w_gate, w_up, w_down, layer, seq):
    n = x2.shape[0]
    tm = min(512, seq)
    tpb = seq // tm
    tile = lambda width: pl.BlockSpec((tm, width), lambda i: (i, 0))
    resident = lambda rows, cols: pl.BlockSpec((None, rows, cols), lambda i: (layer, 0, 0),
                                               pipeline_mode=pl.Buffered(1))
    return pl.pallas_call(
        _outproj_ffn_kernel,
        grid=(n // tm,),
        in_specs=[tile(out_a.shape[1]), tile(out_b.shape[1]), tile(out_c.shape[1]), tile(D_MODEL),
                  _mod_spec(layer, 2, tpb), pl.BlockSpec((1, D_MODEL), lambda i: (0, 0)),
                  _mod_spec(layer, 4, tpb), _mod_spec(layer, 3, tpb), _mod_spec(layer, 5, tpb),
                  resident(D_MIX, D_MODEL), resident(D_MODEL, D_FF), resident(D_MODEL, D_FF),
                  resident(D_FF, D_MODEL)],
        out_specs=tile(D_MODEL),
        out_shape=jax.ShapeDtypeStruct((n, D_MODEL), F32),
        compiler_params=pltpu.CompilerParams(
            dimension_semantics=("arbitrary",), vmem_limit_bytes=VMEM_LIMIT),
        name="outproj_ffn",
    )(out_a, out_b, out_c, x2, mod5, gain, mod5, mod5, mod5, w_out, w_gate, w_up, w_down)


def _head_mean_matrix(size):
    i = np.arange(size) // HEAD_DIM
    return jnp.asarray((i[:, None] == i[None, :]).astype(np.float32) / HEAD_DIM, BF16)


def kernel(x, c, positions, ada_w, ada_b, norm_mix, w_in, attn_q_norm, attn_k_norm, attn_sinks,
           hgrn_lb_logits, hgrn_out_norm, gdn_conv_w, gdn_a_log, gdn_dt_bias, gdn_out_norm, w_out,
           norm_ffn, w_gate, w_up, w_down):
    batch, seq, _ = x.shape
    depth = ada_w.shape[0]
    n = batch * seq
    x2 = x.reshape(n, D_MODEL).astype(F32)

    mod = _modulation(c.astype(F32), ada_w.astype(F32), ada_b.astype(F32))
    mod5 = mod.reshape(depth, 6, batch, 1, D_MODEL)
    cos_t, sin_t = _rope_tables(positions)
    cos3 = cos_t.reshape(batch, seq, LANES)
    sin3 = sin_t.reshape(batch, seq, LANES)
    bd256 = _head_mean_matrix(B_HEADS * HEAD_DIM)

    w_in_b = w_in.astype(BF16)
    w_gate_cols = jnp.pad(w_in[:, :, GATE_COL:].astype(BF16), ((0, 0), (0, 0), (0, D_IN_PAD - D_IN)))
    w_out, w_gate, w_up, w_down = (w.astype(BF16) for w in (w_out, w_gate, w_up, w_down))

    for l in range(depth):
        proj = _inproj(x2, mod5, norm_mix[l].astype(F32).reshape(1, D_MODEL), w_in_b, w_gate_cols, l, seq)
        proj3 = proj.reshape(batch, seq, D_IN_PAD)
        out_a = _attention(proj3, cos3, sin3, attn_q_norm[l], attn_k_norm[l],
                           attn_sinks[l]).reshape(n, -1)
        out_b, out_c = _recurrent_mixers(proj3, hgrn_lb_logits, hgrn_out_norm[l], gdn_conv_w[l],
                                         gdn_a_log[l], gdn_dt_bias[l], gdn_out_norm[l], bd256, l)
        out_b, out_c = out_b.reshape(n, -1), out_c.reshape(n, -1)
        x2 = _outproj_ffn(out_a, out_b, out_c, x2, mod5, norm_ffn[l].astype(F32).reshape(1, D_MODEL),
                          w_out, w_gate, w_up, w_down, l, seq)
    return x2.reshape(batch, seq, D_MODEL).astype(x.dtype)
```

```python
import functools
import math

import numpy as np
import jax
import jax.numpy as jnp
from jax import lax
from jax.experimental import pallas as pl
from jax.experimental.pallas import tpu as pltpu

F32 = jnp.float32
BF16 = jnp.bfloat16

D_MODEL = 1024
HEAD_DIM = 64
A_Q_HEADS = 8
A_KV_HEADS = 2
WINDOW = 128
ROPE_DIM = HEAD_DIM // 4
ROPE_THETA = 500000.0
B_HEADS = 4
C_HEADS = 4
CONV_WIDTH = 4
CHUNK = 64
D_FF = 2816
D_MIX = 1024
D_IN = 2824
NORM_EPS = 1e-6
LOG2E = math.log2(math.e)

LANES = 128
SUBLANES = 8
D_IN_PAD = 2944
GATE_COL = 2816
VMEM_LIMIT = 56 * 1024 * 1024

NN = (((1,), (0,)), ((), ()))
TN = (((0,), (0,)), ((), ()))
BNN = (((2,), (1,)), ((0,), (0,)))
BNT = (((2,), (2,)), ((0,), (0,)))


def _dot(a, b, dims=NN):
    return lax.dot_general(a, b, dims, preferred_element_type=F32)


def _split(x, n):
    parts = []
    r = x
    for i in range(n):
        p = r.astype(BF16)
        parts.append(p)
        if i + 1 < n:
            r = r - p.astype(F32)
    return parts


def _dot3(a, b, dims=NN):
    a1, a2 = _split(a, 2)
    b1, b2 = _split(b, 2)
    return _dot(a1, b1, dims) + _dot(a1, b2, dims) + _dot(a2, b1, dims)


def _silu(x):
    h = 0.5 * x
    return h + h * jnp.tanh(h)


def _blockdiag(x):
    heads = x.shape[-1] // HEAD_DIM
    lane_head = lax.broadcasted_iota(jnp.int32, x.shape, x.ndim - 1) // HEAD_DIM
    zero = jnp.zeros_like(x)
    return jnp.concatenate([jnp.where(lane_head == i, x, zero) for i in range(heads)], axis=x.ndim - 2)


def _blockdiag_t(x):
    heads = x.shape[-1] // HEAD_DIM
    r = x.shape[1]
    xt = jnp.swapaxes(jnp.concatenate([x] * heads, axis=1), 1, 2)
    row_head = lax.broadcasted_iota(jnp.int32, xt.shape, 1) // HEAD_DIM
    col_head = lax.broadcasted_iota(jnp.int32, xt.shape, 2) // r
    return jnp.where(row_head == col_head, xt, jnp.zeros_like(xt))


def _head_rmsnorm(o, bd, gain):
    return o * lax.rsqrt(_dot(jnp.square(o).astype(BF16), bd) + NORM_EPS) * gain


def _log1p_exp_neg_abs(x):
    return jnp.log(1.0 + jnp.exp(-jnp.abs(x)))


def _mod_kernel(c_ref, w_ref, b_ref, o_ref):
    c = c_ref[...]
    o_ref[...] = _dot3(_silu(c), w_ref[...]) + b_ref[...]


def _modulation(c, ada_w, ada_b):
    depth = ada_w.shape[0]
    b = c.shape[0]
    return pl.pallas_call(
        _mod_kernel,
        grid=(depth, 6),
        in_specs=[
            pl.BlockSpec((b, D_MODEL), lambda l, k: (0, 0)),
            pl.BlockSpec((None, D_MODEL, D_MODEL), lambda l, k: (l, 0, k)),
            pl.BlockSpec((None, None, 1, D_MODEL), lambda l, k: (l, k, 0, 0)),
        ],
        out_specs=pl.BlockSpec((None, None, b, D_MODEL), lambda l, k: (l, k, 0, 0)),
        out_shape=jax.ShapeDtypeStruct((depth, 6, b, D_MODEL), F32),
        compiler_params=pltpu.CompilerParams(
            dimension_semantics=("arbitrary", "arbitrary"), vmem_limit_bytes=VMEM_LIMIT),
        name="modulation",
    )(c, ada_w, ada_b.reshape(depth, 6, 1, D_MODEL))


def _rope_kernel(pos_ref, inv_ref, sgn_ref, sel_ref, cos_ref, sin_ref):
    rows = sel_ref.shape[0]
    pos = jnp.broadcast_to(pos_ref[...].astype(F32), (rows, pos_ref.shape[1]))
    posx = None
    for piece in _split(pos, 3):
        t = _dot(piece, sel_ref[...], TN)
        posx = t if posx is None else posx + t
    ang = posx * inv_ref[...]
    cos_ref[...] = jnp.cos(ang)
    sin_ref[...] = jnp.sin(ang) * sgn_ref[...]


def _rope_tables(positions):
    n = positions.size
    tm = min(2048, n)
    half = ROPE_DIM // 2
    inv_freq = ROPE_THETA ** (-jnp.arange(half, dtype=F32) * 2.0 / ROPE_DIM)
    lane = np.arange(LANES) % HEAD_DIM
    inv_lane = jnp.where(lane < ROPE_DIM, inv_freq[lane % half], 0.0).reshape(1, LANES).astype(F32)
    sgn_lane = jnp.asarray(np.where(lane < half, -1.0, np.where(lane < ROPE_DIM, 1.0, 0.0)),
                           F32).reshape(1, LANES)
    sel_rows = 2 * SUBLANES
    sel = jnp.asarray(np.arange(sel_rows)[:, None] == 0, BF16) * jnp.ones((1, LANES), BF16)
    return pl.pallas_call(
        _rope_kernel,
        grid=(n // tm,),
        in_specs=[
            pl.BlockSpec((None, 1, tm), lambda i: (i, 0, 0)),
            pl.BlockSpec((1, LANES), lambda i: (0, 0)),
            pl.BlockSpec((1, LANES), lambda i: (0, 0)),
            pl.BlockSpec((sel_rows, LANES), lambda i: (0, 0)),
        ],
        out_specs=[pl.BlockSpec((tm, LANES), lambda i: (i, 0))] * 2,
        out_shape=[jax.ShapeDtypeStruct((n, LANES), F32)] * 2,
        compiler_params=pltpu.CompilerParams(dimension_semantics=("arbitrary",)),
        name="rope_tables",
    )(positions.reshape(n // tm, 1, tm), inv_lane, sgn_lane, sel)


def _modulated_norm(x, gain, scale, shift):
    ms = jnp.mean(x * x, axis=-1, keepdims=True)
    return x * lax.rsqrt(ms + NORM_EPS) * gain * (1.0 + scale) + shift


def _inproj_kernel(x_ref, g_ref, sc_ref, sh_ref, w_ref, wgate_ref, o_ref):
    hb = _modulated_norm(x_ref[...], g_ref[...], sc_ref[...], sh_ref[...]).astype(BF16)
    o_ref[:, 0:GATE_COL] = _dot(hb, w_ref[...])
    o_ref[:, GATE_COL:D_IN_PAD] = _dot(hb, wgate_ref[...])


def _mod_spec(layer, which, tiles_per_batch):
    return pl.BlockSpec((None, None, None, 1, D_MODEL),
                        lambda i: (layer, which, i // tiles_per_batch, 0, 0))


def _inproj(x2, mod5, gain, w_in, w_gate_cols, layer, seq):
    n = x2.shape[0]
    tm = min(1024, seq)
    tpb = seq // tm
    return pl.pallas_call(
        _inproj_kernel,
        grid=(n // tm,),
        in_specs=[
            pl.BlockSpec((tm, D_MODEL), lambda i: (i, 0)),
            pl.BlockSpec((1, D_MODEL), lambda i: (0, 0)),
            _mod_spec(layer, 1, tpb),
            _mod_spec(layer, 0, tpb),
            pl.BlockSpec((None, D_MODEL, GATE_COL), lambda i: (layer, 0, 0), pipeline_mode=pl.Buffered(1)),
            pl.BlockSpec((None, D_MODEL, LANES), lambda i: (layer, 0, 0), pipeline_mode=pl.Buffered(1)),
        ],
        out_specs=pl.BlockSpec((tm, D_IN_PAD), lambda i: (i, 0)),
        out_shape=jax.ShapeDtypeStruct((n, D_IN_PAD), F32),
        compiler_params=pltpu.CompilerParams(
            dimension_semantics=("arbitrary",), vmem_limit_bytes=VMEM_LIMIT),
        name="inproj",
    )(x2, gain, mod5, mod5, w_in, w_gate_cols)


def _attn_consts():
    i = np.arange(LANES)
    within = i % HEAD_DIM
    half = ROPE_DIM // 2
    src = np.where(within < half, i + half, np.where(within < ROPE_DIM, i - half, -1))
    rot = (i[:, None] == src[None, :]).astype(np.float32)
    mean = ((i // HEAD_DIM)[:, None] == (i // HEAD_DIM)[None, :]).astype(np.float32) / HEAD_DIM
    zero = np.zeros((LANES, LANES), np.float32)
    norm_rot = np.block([[mean, zero], [zero, rot]])
    swap = (i[:, None] == ((i + HEAD_DIM) % LANES)[None, :]).astype(np.float32)
    return jnp.asarray(norm_rot, BF16), jnp.asarray(swap, BF16)


def _attn_kernel(sink_ref, q_ref, k_ref, v_ref, cos_ref, sin_ref, qg_ref, kg_ref, nr_ref,
                 swap_ref, o_ref, kvar_ref, vvar_ref):
    n = pl.program_id(0)
    w = WINDOW
    nb = q_ref.shape[0]
    rows = nb * w

    @pl.when(n == 0)
    def _():
        kvar_ref[:, :, 0:w, :] = jnp.zeros((4, nb, w, LANES), BF16)
        vvar_ref[:, :, 0:w, :] = jnp.zeros((4, nb, w, LANES), BF16)

    @pl.when(n > 0)
    def _():
        kvar_ref[:, :, 0:w, :] = kvar_ref[:, :, w:2 * w, :]
        vvar_ref[:, :, 0:w, :] = vvar_ref[:, :, w:2 * w, :]

    cos = cos_ref[...].reshape(rows, LANES)
    sin = sin_ref[...].reshape(rows, LANES)
    norm_rot = nr_ref[...]
    swap = swap_ref[...]
    lo = lax.broadcasted_iota(jnp.int32, (rows, LANES), 1) < HEAD_DIM

    def norm_rope(xp, gain):
        xg = xp * gain
        res = _dot(jnp.concatenate([jnp.square(xp).astype(BF16), xg.astype(BF16)], axis=1), norm_rot)
        r = lax.rsqrt(res[:, 0:LANES] + NORM_EPS)
        return (xg * cos + res[:, LANES:2 * LANES] * sin) * r

    def variants(x):
        xb = x.astype(BF16)
        xs = _dot(xb, swap).astype(BF16)
        zero = jnp.zeros_like(xb)
        return [jnp.where(lo, xb, zero), jnp.where(lo, zero, xs),
                jnp.where(lo, xs, zero), jnp.where(lo, zero, xb)]

    kn = norm_rope(k_ref[...].reshape(rows, LANES), kg_ref[...])
    for i, t in enumerate(variants(kn)):
        kvar_ref[i, :, w:2 * w, :] = t.reshape(nb, w, LANES)
    for i, t in enumerate(variants(v_ref[...].reshape(rows, LANES))):
        vvar_ref[i, :, w:2 * w, :] = t.reshape(nb, w, LANES)

    qi = lax.broadcasted_iota(jnp.int32, (1, w, 2 * w), 1)
    kj = lax.broadcasted_iota(jnp.int32, (1, w, 2 * w), 2)
    delta = qi + w - kj
    valid = (delta >= 0) & (delta < w) & ((n * w + kj - w) >= 0)
    bias = jnp.where(valid, 0.0, -jnp.inf).astype(F32)
    lo3 = lax.broadcasted_iota(jnp.int32, (1, w, LANES), 2) < HEAD_DIM

    for p in range(A_Q_HEADS // 2):
        g = (2 * p) // (A_Q_HEADS // A_KV_HEADS)
        qn = norm_rope(q_ref[:, :, p * LANES:(p + 1) * LANES].reshape(rows, LANES), qg_ref[...])
        qb = (qn * (HEAD_DIM ** -0.5 * LOG2E)).astype(BF16).reshape(nb, w, LANES)
        num, ms, sums = None, [], []
        for par in range(2):
            sink = sink_ref[2 * p + par] * LOG2E
            s = _dot(qb, kvar_ref[2 * g + par], BNT) + bias
            m = jnp.maximum(jnp.max(s, axis=-1, keepdims=True), sink)
            e = jnp.exp2(s - m)
            ms.append(m)
            sums.append(jnp.sum(e, axis=-1, keepdims=True))
            pv = _dot(e.astype(BF16), vvar_ref[2 * g + par], BNN)
            num = pv if num is None else num + pv
        sink_pair = jnp.where(lo3, sink_ref[2 * p] * LOG2E, sink_ref[2 * p + 1] * LOG2E)
        den = jnp.where(lo3, sums[0], sums[1]) + jnp.exp2(sink_pair - jnp.where(lo3, ms[0], ms[1]))
        o_ref[:, :, p * LANES:(p + 1) * LANES] = (num / den).astype(o_ref.dtype)


def _attention(proj3, cos_t, sin_t, q_norm, k_norm, sinks):
    batch, seq, _ = proj3.shape
    nblk = seq // WINDOW
    qw = A_Q_HEADS * HEAD_DIM
    qg = jnp.tile(q_norm.astype(F32), 2).reshape(1, LANES)
    kg = jnp.tile(k_norm.astype(F32), 2).reshape(1, LANES)
    const = lambda shape: pl.BlockSpec(shape, lambda i: (0, 0))
    return pl.pallas_call(
        _attn_kernel,
        grid=(nblk,),
        in_specs=[
            pl.BlockSpec(memory_space=pltpu.SMEM),
            pl.BlockSpec((batch, WINDOW, qw), lambda i: (0, i, 0)),
            pl.BlockSpec((batch, WINDOW, LANES), lambda i: (0, i, qw // LANES)),
            pl.BlockSpec((batch, WINDOW, LANES), lambda i: (0, i, qw // LANES + 1)),
            pl.BlockSpec((batch, WINDOW, LANES), lambda i: (0, i, 0)),
            pl.BlockSpec((batch, WINDOW, LANES), lambda i: (0, i, 0)),
            const((1, LANES)), const((1, LANES)), const((2 * LANES, 2 * LANES)), const((LANES, LANES)),
        ],
        out_specs=pl.BlockSpec((batch, WINDOW, qw), lambda i: (0, i, 0)),
        out_shape=jax.ShapeDtypeStruct((batch, seq, qw), BF16),
        scratch_shapes=[pltpu.VMEM((4, batch, 2 * WINDOW, LANES), BF16),
                        pltpu.VMEM((4, batch, 2 * WINDOW, LANES), BF16)],
        compiler_params=pltpu.CompilerParams(
            dimension_semantics=("arbitrary",), vmem_limit_bytes=VMEM_LIMIT),
        name="attention",
    )(sinks.astype(F32), proj3, proj3, proj3, cos_t, sin_t, qg, kg, *_attn_consts())


def _hgrn_consts():
    c = CHUNK
    t = np.arange(c)[:, None]
    r = np.arange(c)[None, :]
    mall = np.concatenate([(r <= t), (r > t)], axis=0).astype(np.float32)
    s = np.arange(LANES)[None, :] % c
    x = t ^ s
    lvl = np.where(t > s, np.floor(np.log2(np.maximum(x, 1))).astype(np.int32),
                   np.where(t == s, -1, -2)).astype(np.int32)
    return jnp.asarray(mall, BF16), jnp.asarray(lvl, jnp.int32)


def _head_block_mask(nh):
    ri = lax.broadcasted_iota(jnp.int32, (nh, nh), 0) // HEAD_DIM
    ci = lax.broadcasted_iota(jnp.int32, (nh, nh), 1) // HEAD_DIM
    return ri == ci


def _hgrn_kernel(q_ref, z_ref, v_ref, gt_ref, lbl_ref, gain_ref, mall_ref, lvl_ref, bd_ref,
                 o_ref, st_ref, *, layer):
    c = CHUNK
    nh = B_HEADS * HEAD_DIM
    batch, span, _ = q_ref.shape
    nb = batch
    rows = nb * c

    @pl.when(pl.program_id(0) == 0)
    def _():
        st_ref[...] = jnp.zeros_like(st_ref)

    q = q_ref[...].reshape(rows, nh)
    z = z_ref[...].reshape(rows, nh)
    v = v_ref[...].reshape(rows, nh)

    log_sig = jnp.minimum(z, 0.0) - _log1p_exp_neg_abs(z)
    if layer == 0:
        lf = log_sig
        kk = jax.nn.sigmoid(-z)
    else:
        lg = lbl_ref[...]
        e = jnp.exp(lg - jnp.max(lg, axis=0, keepdims=True))
        sm = e / jnp.sum(e, axis=0, keepdims=True)
        lb = sm[1:2, :]
        for j in range(2, layer + 1):
            lb = lb + sm[j:j + 1, :]
        a = jnp.log(lb)
        b = jnp.log1p(-lb) + log_sig
        lf = jnp.maximum(a, b) + _log1p_exp_neg_abs(a - b)
        kk = (1.0 - lb) * jax.nn.sigmoid(-z)

    lf3 = lf.reshape(nb, c, nh)
    pieces = _split(lf3, 3)
    m_cr = jnp.broadcast_to(mall_ref[0:2 * c, :][None], (nb, 2 * c, c))
    cr = _dot(m_cr, pieces[0], BNN) + _dot(m_cr, pieces[1], BNN) + _dot(m_cr, pieces[2], BNN)
    cum = cr[:, 0:c]
    rest = cr[:, c:2 * c]

    trow_full = lax.broadcasted_iota(jnp.int32, (1, c, nh), 1)
    cum8 = cum.reshape(nb, c // SUBLANES, SUBLANES, nh)
    sub = lax.broadcasted_iota(jnp.int32, (1, 1, SUBLANES, nh), 2)
    dlv = [jnp.where((trow_full & 1) == 1, lf3, jnp.zeros_like(lf3))]
    for l in range(1, 6):
        m = 1 << l
        if 2 * m > SUBLANES:
            ref = jnp.concatenate(
                [jnp.broadcast_to(cum[:, j + m - 1:j + m, :], (nb, 2 * m, nh)) for j in range(0, c, 2 * m)],
                axis=1)
        elif 2 * m == SUBLANES:
            ref = jnp.broadcast_to(cum8[:, :, m - 1:m, :], cum8.shape).reshape(nb, c, nh)
        else:
            ref = jnp.where(sub < 2 * m, jnp.broadcast_to(cum8[:, :, m - 1:m, :], cum8.shape),
                            jnp.broadcast_to(cum8[:, :, 3 * m - 1:3 * m, :], cum8.shape)).reshape(nb, c, nh)
        diff = cum - ref
        dlv.append(jnp.where(((trow_full >> l) & 1) == 1, diff, -diff))

    q3 = q.reshape(nb, c, nh)
    k3 = kk.reshape(nb, c, nh)
    vb = v.reshape(nb, c, nh).astype(BF16)
    lvl = lvl_ref[...][None]
    trow = lax.broadcasted_iota(jnp.int32, (1, c, LANES), 1)
    o_intra = []
    for p in range(B_HEADS // 2):
        sl = slice(p * LANES, (p + 1) * LANES)
        qp, kp = q3[:, :, sl], k3[:, :, sl]
        pm = jnp.where(lvl == -1, _dot(qp.astype(BF16), _blockdiag_t(kp.astype(BF16)), BNN),
                       jnp.zeros((nb, c, LANES), F32))
        for l in range(6):
            is_q = ((trow >> l) & 1) == 1
            xl = (jnp.where(is_q, qp, kp) * jnp.exp(dlv[l][:, :, sl])).astype(BF16)
            pm = jnp.where(lvl == l, _dot(xl, _blockdiag_t(xl), BNN), pm)
        o_intra.append(_dot(pm.astype(BF16), _blockdiag(vb[:, :, sl]), BNN))
    o_intra = jnp.concatenate(o_intra, axis=2)

    q_dec = (q3 * jnp.exp(cum)).astype(BF16)
    k_dec = (k3 * jnp.exp(rest)).astype(BF16)
    dec_last = jnp.exp(cum[:, c - 1:c, :])
    blk = _head_block_mask(nh)[None]
    st = st_ref[...]
    st_t = jnp.swapaxes(st.astype(BF16), 1, 2)
    o_inter = _dot(q_dec, jnp.where(blk, st_t, jnp.zeros_like(st_t)), BNN)
    upd = _dot(jnp.swapaxes(vb, 1, 2), k_dec, BNN)
    st_ref[...] = st * dec_last + jnp.where(blk, upd, jnp.zeros_like(upd))
    o = (o_intra + o_inter).reshape(rows, nh)

    y = _head_rmsnorm(o, bd_ref[...], gain_ref[...])
    o_ref[...] = (y * _silu(gt_ref[...].reshape(rows, nh))).reshape(batch, span, nh).astype(o_ref.dtype)


def _hgrn_operands(proj3, lb_logits, gain, bd256):
    batch = proj3.shape[0]
    nh = B_HEADS * HEAD_DIM
    depth = lb_logits.shape[0]
    mall, lvl = _hgrn_consts()
    col0 = (A_Q_HEADS + 2 * A_KV_HEADS) * HEAD_DIM // nh
    const = lambda shape: pl.BlockSpec(shape, lambda i: (0, 0))
    in_specs = ([pl.BlockSpec((batch, CHUNK, nh), lambda i, k=k: (0, i, col0 + k)) for k in range(4)]
                + [const((depth, nh)), const((1, nh)), const(mall.shape), const(lvl.shape), const((nh, nh))])
    operands = (proj3, proj3, proj3, proj3, lb_logits.astype(F32),
                jnp.tile(gain.astype(F32), B_HEADS).reshape(1, nh), mall, lvl, bd256)
    return in_specs, operands, [pltpu.VMEM((batch, nh, nh), F32)]


def _gdn_consts():
    c = CHUNK
    t = np.arange(c)[:, None]
    r = np.arange(c)[None, :]
    tri2 = np.concatenate([(r <= t), (r > t)], axis=0).astype(np.float32)
    return jnp.asarray(tri2, BF16)


def _gdn_kernel(cq_ref, ck_ref, cv_ref, cg_ref, gate_ref, cw_ref, alog_ref, dt_ref, gain_ref,
                tri_ref, bd_ref, o_ref, xbuf_ref, s_ref):
    c = CHUNK
    nh = C_HEADS * HEAD_DIM
    batch, span, _ = cq_ref.shape
    nb = batch
    rows = nb * c
    tail = SUBLANES

    @pl.when(pl.program_id(0) == 0)
    def _():
        s_ref[...] = jnp.zeros_like(s_ref)
        xbuf_ref[:, 0:tail, :] = jnp.zeros((batch, tail, 3 * nh), F32)

    xbuf_ref[:, tail:tail + span, 0:nh] = cq_ref[...]
    xbuf_ref[:, tail:tail + span, nh:2 * nh] = ck_ref[...]
    xbuf_ref[:, tail:tail + span, 2 * nh:3 * nh] = cv_ref[...]
    w = cw_ref[...]
    groups = span // tail
    x9 = xbuf_ref[...].reshape(batch, groups + 1, tail, 3 * nh)
    sub = lax.broadcasted_iota(jnp.int32, (1, 1, tail, 3 * nh), 2)
    y = x9[:, 1:] * w[CONV_WIDTH - 1:CONV_WIDTH, :][None, None]
    for k in range(1, CONV_WIDTH):
        rk = pltpu.roll(x9, k, 2)
        shifted = jnp.where(sub >= k, rk[:, 1:], rk[:, :groups])
        y = y + shifted * w[CONV_WIDTH - 1 - k:CONV_WIDTH - k, :][None, None]
    xbuf_ref[:, 0:tail, :] = xbuf_ref[:, span:span + tail, :]
    y = _silu(y).reshape(rows, 3 * nh)
    q, k, v = y[:, 0:nh], y[:, nh:2 * nh], y[:, 2 * nh:3 * nh]

    bd = bd_ref[...]
    q = q * (lax.rsqrt(_dot(jnp.square(q).astype(BF16), bd) + NORM_EPS / HEAD_DIM) * (1.0 / HEAD_DIM))
    k = k * (lax.rsqrt(_dot(jnp.square(k).astype(BF16), bd) + NORM_EPS / HEAD_DIM) * (HEAD_DIM ** -0.5))

    gb = gate_ref[...].reshape(rows, LANES)
    head = lax.broadcasted_iota(jnp.int32, (rows, nh), 1) // HEAD_DIM

    def per_head(col0):
        out = jnp.broadcast_to(gb[:, col0:col0 + 1], (rows, nh))
        for h in range(1, C_HEADS):
            out = jnp.where(head == h, jnp.broadcast_to(gb[:, col0 + h:col0 + h + 1], (rows, nh)), out)
        return out

    beta = jax.nn.sigmoid(per_head(0))
    xg = per_head(C_HEADS) + dt_ref[...]
    g = -jnp.exp(alog_ref[...]) * (jnp.maximum(xg, 0.0) + _log1p_exp_neg_abs(xg))

    g3 = g.reshape(nb, c, nh)
    tri2 = jnp.broadcast_to(tri_ref[...][None], (nb, 2 * c, c))
    g_pieces = _split(g3, 3)
    gsum = None
    for piece in g_pieces:
        t = _dot(tri2, piece, BNN)
        gsum = t if gsum is None else gsum + t
    gcum = gsum[:, 0:c]
    grest = gsum[:, c:2 * c]
    exp_g = jnp.exp(gcum)
    q3 = q.reshape(nb, c, nh)
    k3 = k.reshape(nb, c, nh)
    beta3 = beta.reshape(nb, c, nh)
    kb = k3 * beta3
    vb = v.reshape(nb, c, nh) * beta3
    kbg = kb * exp_g

    ti = lax.broadcasted_iota(jnp.int32, (1, c, nh), 1)
    si = lax.broadcasted_iota(jnp.int32, (1, c, nh), 2) & (HEAD_DIM - 1)
    incl = ti >= si
    strict = ti > si
    zero = jnp.zeros((nb, c, nh), F32)
    eye = jnp.where(ti == si, 1.0, 0.0).astype(F32)
    same16 = (ti >> 4) == (si >> 4)

    def pmm(a, b):
        return _dot(a.astype(BF16), _blockdiag(b.astype(BF16)), BNN)

    def pmm2(a, y):
        yb = y.astype(BF16)
        rhs = jnp.concatenate([_blockdiag(yb[:, :, 0:nh]), _blockdiag(yb[:, :, nh:2 * nh])], axis=2)
        return _dot(a.astype(BF16), rhs, BNN)

    ldiff = None
    for piece in g_pieces[0:2]:
        t = _dot(tri2[:, 0:c], jnp.where(strict, piece, jnp.zeros_like(piece)), BNN)
        ldiff = t if ldiff is None else ldiff + t
    lmat = jnp.where(incl, jnp.exp(jnp.where(incl, ldiff, zero)), zero)
    kbd_t = _blockdiag_t(k3.astype(BF16))
    amat = jnp.where(strict, _dot(kb.astype(BF16), kbd_t, BNN) * lmat, zero)
    dmat = jnp.where(same16, amat, zero)
    noff = amat - dmat
    bm = -dmat
    b2 = pmm(bm, bm)
    b4 = pmm(b2, b2)
    b8 = pmm(b4, b4)
    td = eye + bm
    td = td + pmm(td, b2)
    td = td + pmm(td, b4)
    td = td + pmm(td, b8)
    mm = pmm(td, noff)
    y0 = pmm2(td, jnp.concatenate([vb, kbg], axis=2))
    m2 = pmm(mm, mm)
    y1 = y0 + pmm2(m2, y0)
    y2 = y1 - pmm2(mm, y1)
    u = y2[:, :, 0:nh]
    wmat = y2[:, :, nh:2 * nh].astype(BF16)
    qk = jnp.where(incl, _dot(q3.astype(BF16), kbd_t, BNN) * lmat, zero).astype(BF16)

    q_dec = (q3 * exp_g).astype(BF16)
    k_dec = (k3 * jnp.exp(grest)).astype(BF16)
    dec_last = jnp.exp(gcum[:, c - 1:c, :])
    blk = _head_block_mask(nh)[None]
    s = s_ref[...]
    sb = s.astype(BF16)
    v_new = (u - _dot(wmat, sb, BNN)).astype(BF16)
    o = _dot(q_dec, sb, BNN) + _dot(qk, _blockdiag(v_new), BNN)
    upd = _dot(jnp.swapaxes(k_dec, 1, 2), v_new, BNN)
    s_ref[...] = s * dec_last + jnp.where(blk, upd, jnp.zeros_like(upd))
    o = o.reshape(rows, nh)

    yo = _head_rmsnorm(o, bd, gain_ref[...])
    o_ref[...] = (yo * _silu(cg_ref[...].reshape(rows, nh))).reshape(batch, span, nh).astype(o_ref.dtype)


def _gdn_operands(proj3, conv_w, a_log, dt_bias, gain, bd256):
    batch = proj3.shape[0]
    nh = C_HEADS * HEAD_DIM
    tri2 = _gdn_consts()
    col0 = ((A_Q_HEADS + 2 * A_KV_HEADS) * HEAD_DIM + 4 * nh) // nh
    const = lambda shape: pl.BlockSpec(shape, lambda i: (0, 0))
    rep = lambda a: jnp.repeat(a.astype(F32), HEAD_DIM).reshape(1, nh)
    in_specs = ([pl.BlockSpec((batch, CHUNK, nh), lambda i, k=k: (0, i, col0 + k)) for k in range(4)]
                + [pl.BlockSpec((batch, CHUNK, LANES), lambda i: (0, i, GATE_COL // LANES)),
                   const((CONV_WIDTH, 3 * nh)), const((1, nh)), const((1, nh)), const((1, nh)),
                   const(tri2.shape), const((nh, nh))])
    operands = (proj3, proj3, proj3, proj3, proj3, conv_w.astype(F32), rep(a_log), rep(dt_bias),
                jnp.tile(gain.astype(F32), C_HEADS).reshape(1, nh), tri2, bd256)
    scratch = [pltpu.VMEM((batch, CHUNK + SUBLANES, 3 * nh), F32), pltpu.VMEM((batch, nh, nh), F32)]
    return in_specs, operands, scratch


def _recurrent_kernel(*refs, layer, n_hgrn_in, n_gdn_in):
    h_in = refs[0:n_hgrn_in]
    g_in = refs[n_hgrn_in:n_hgrn_in + n_gdn_in]
    o_h, o_g, st_h, xbuf, s_g = refs[n_hgrn_in + n_gdn_in:]
    _hgrn_kernel(*h_in, o_h, st_h, layer=layer)
    _gdn_kernel(*g_in, o_g, xbuf, s_g)


def _recurrent_mixers(proj3, lb_logits, hgrn_gain, conv_w, a_log, dt_bias, gdn_gain, bd256, layer):
    batch, seq, _ = proj3.shape
    nh = B_HEADS * HEAD_DIM
    h_specs, h_ops, h_scratch = _hgrn_operands(proj3, lb_logits, hgrn_gain, bd256)
    g_specs, g_ops, g_scratch = _gdn_operands(proj3, conv_w, a_log, dt_bias, gdn_gain, bd256)
    out_spec = pl.BlockSpec((batch, CHUNK, nh), lambda i: (0, i, 0))
    return pl.pallas_call(
        functools.partial(_recurrent_kernel, layer=layer, n_hgrn_in=len(h_ops), n_gdn_in=len(g_ops)),
        grid=(seq // CHUNK,),
        in_specs=h_specs + g_specs,
        out_specs=[out_spec, out_spec],
        out_shape=[jax.ShapeDtypeStruct((batch, seq, nh), BF16)] * 2,
        scratch_shapes=h_scratch + g_scratch,
        compiler_params=pltpu.CompilerParams(
            dimension_semantics=("arbitrary",), vmem_limit_bytes=VMEM_LIMIT),
        name="recurrent_mixers",
    )(*h_ops, *g_ops)


def _outproj_ffn_kernel(a_ref, b_ref, c_ref, x_ref, gtm_ref, g_ref, sc_ref, sh_ref, gtf_ref,
                        wo_ref, wg_ref, wu_ref, wd_ref, o_ref):
    wa = A_Q_HEADS * HEAD_DIM
    wb = wa + B_HEADS * HEAD_DIM
    y = (_dot(a_ref[...], wo_ref[0:wa, :]) + _dot(b_ref[...], wo_ref[wa:wb, :])
         + _dot(c_ref[...], wo_ref[wb:D_MIX, :]))
    x1 = x_ref[...] + gtm_ref[...] * y
    h = _modulated_norm(x1, g_ref[...], sc_ref[...], sh_ref[...]).astype(BF16)
    act = (_silu(_dot(h, wg_ref[...])) * _dot(h, wu_ref[...])).astype(BF16)
    o_ref[...] = x1 + gtf_ref[...] * _dot(act, wd_ref[...])


def _outproj_ffn(out_a, out_b, out_c, x2, mod5, gain, w_out, w_gate, w_up, w_down, layer, seq):
    n = x2.shape[0]
    tm = min(512, seq)
    tpb = seq // tm
    tile = lambda width: pl.BlockSpec((tm, width), lambda i: (i, 0))
    resident = lambda rows, cols: pl.BlockSpec((None, rows, cols), lambda i: (layer, 0, 0),
                                               pipeline_mode=pl.Buffered(1))
    return pl.pallas_call(
        _outproj_ffn_kernel,
        grid=(n // tm,),
        in_specs=[tile(out_a.shape[1]), tile(out_b.shape[1]), tile(out_c.shape[1]), tile(D_MODEL),
                  _mod_spec(layer, 2, tpb), pl.BlockSpec((1, D_MODEL), lambda i: (0, 0)),
                  _mod_spec(layer, 4, tpb), _mod_spec(layer, 3, tpb), _mod_spec(layer, 5, tpb),
                  resident(D_MIX, D_MODEL), resident(D_MODEL, D_FF), resident(D_MODEL, D_FF),
                  resident(D_FF, D_MODEL)],
        out_specs=tile(D_MODEL),
        out_shape=jax.ShapeDtypeStruct((n, D_MODEL), F32),
        compiler_params=pltpu.CompilerParams(
            dimension_semantics=("arbitrary",), vmem_limit_bytes=VMEM_LIMIT),
        name="outproj_ffn",
    )(out_a, out_b, out_c, x2, mod5, gain, mod5, mod5, mod5, w_out, w_gate, w_up, w_down)


def _head_mean_matrix(size):
    i = np.arange(size) // HEAD_DIM
    return jnp.asarray((i[:, None] == i[None, :]).astype(np.float32) / HEAD_DIM, BF16)


def kernel(x, c, positions, ada_w, ada_b, norm_mix, w_in, attn_q_norm, attn_k_norm, attn_sinks,
           hgrn_lb_logits, hgrn_out_norm, gdn_conv_w, gdn_a_log, gdn_dt_bias, gdn_out_norm, w_out,
           norm_ffn, w_gate, w_up, w_down):
    batch, seq, _ = x.shape
    depth = ada_w.shape[0]
    n = batch * seq
    x2 = x.reshape(n, D_MODEL).astype(F32)

    mod = _modulation(c.astype(F32), ada_w.astype(F32), ada_b.astype(F32))
    mod5 = mod.reshape(depth, 6, batch, 1, D_MODEL)
    cos_t, sin_t = _rope_tables(positions)
    cos3 = cos_t.reshape(batch, seq, LANES)
    sin3 = sin_t.reshape(batch, seq, LANES)
    bd256 = _head_mean_matrix(B_HEADS * HEAD_DIM)

    w_in_b = w_in.astype(BF16)
    w_gate_cols = jnp.pad(w_in[:, :, GATE_COL:].astype(BF16), ((0, 0), (0, 0), (0, D_IN_PAD - D_IN)))
    w_out, w_gate, w_up, w_down = (w.astype(BF16) for w in (w_out, w_gate, w_up, w_down))

    for l in range(depth):
        proj = _inproj(x2, mod5, norm_mix[l].astype(F32).reshape(1, D_MODEL), w_in_b, w_gate_cols, l, seq)
        proj3 = proj.reshape(batch, seq, D_IN_PAD)
        out_a = _attention(proj3, cos3, sin3, attn_q_norm[l], attn_k_norm[l],
                           attn_sinks[l]).reshape(n, -1)
        out_b, out_c = _recurrent_mixers(proj3, hgrn_lb_logits, hgrn_out_norm[l], gdn_conv_w[l],
                                         gdn_a_log[l], gdn_dt_bias[l], gdn_out_norm[l], bd256, l)
        out_b, out_c = out_b.reshape(n, -1), out_c.reshape(n, -1)
        x2 = _outproj_ffn(out_a, out_b, out_c, x2, mod5, norm_ffn[l].astype(F32).reshape(1, D_MODEL),
                          w_out, w_gate, w_up, w_down, l, seq)
    return x2.reshape(batch, seq, D_MODEL).astype(x.dtype)
```
